```python
import math
import jax, jax.numpy as jnp
from jax import lax
import numpy as np

D_MODEL = 1024
BATCH = 8
SEQ = 2048
DEPTH = 1

N_HEADS_ATTN = 8
N_KV_HEADS = 2
HEAD_DIM_ATTN = 64
ATTN_GROUP = N_HEADS_ATTN // N_KV_HEADS
WINDOW = 128
ATTN_BLOCK = WINDOW
N_REL_BUCKETS = 32
REL_MAX_DISTANCE = 128
N_HEADS_MLSTM = 4
HEAD_DIM_MLSTM = 128
MLSTM_CHUNK = 64
CONV_WIDTH = 5
ATTN_Q_W = N_HEADS_ATTN * HEAD_DIM_ATTN
ATTN_KV_W = N_KV_HEADS * HEAD_DIM_ATTN
MLSTM_W = N_HEADS_MLSTM * HEAD_DIM_MLSTM
MLSTM_GATE_W = 2 * 2 * N_HEADS_MLSTM
MERGE_W = 2 * D_MODEL
IN_SPLITS = (ATTN_Q_W, ATTN_KV_W, ATTN_KV_W, MLSTM_W, MLSTM_W, MLSTM_W, MLSTM_W, MLSTM_GATE_W, MERGE_W)
D_IN_PROJ = sum(IN_SPLITS)
N_GROUPS = 4
EXPERTS_PER_GROUP = 8
N_EXPERTS = N_GROUPS * EXPERTS_PER_GROUP
TOP_K_EXPERT = 2
D_EXPERT = 512
MOE_BLOCK = 128
RMS_EPS = 1e-6
NEG_INF = -1e30

kernel_name = "hybrid_gqa_mlstm_hiermoe_encoder"


def rmsnorm(x, g):
    xf = x.astype(jnp.float32)
    y = xf * lax.rsqrt(jnp.mean(xf * xf, axis=-1, keepdims=True) + RMS_EPS)
    return (y * g.astype(jnp.float32)).astype(x.dtype)


def split_cols(t, sizes):
    outs, start = [], 0
    for s in sizes:
        outs.append(t[..., start:start + s])
        start += s
    return outs


def t5_bucket(rel):
    half = N_REL_BUCKETS // 2
    max_exact = half // 2
    sign = jnp.where(rel > 0, half, 0)
    n = jnp.abs(rel)
    nf = jnp.maximum(n, 1).astype(jnp.float32)
    large = max_exact + (jnp.log(nf / max_exact) / math.log(REL_MAX_DISTANCE / max_exact)
                         * (half - max_exact)).astype(jnp.int32)
    large = jnp.minimum(large, half - 1)
    return sign + jnp.where(n < max_exact, n, large)


def band_windows(t, nb):
    B = t.shape[0]
    tp = jnp.pad(t, ((0, 0), (ATTN_BLOCK, ATTN_BLOCK), (0, 0), (0, 0)))
    tp = tp.reshape(B, nb + 2, ATTN_BLOCK, t.shape[2], t.shape[3])
    return jnp.concatenate([tp[:, :-2], tp[:, 1:-1], tp[:, 2:]], axis=2)


def windowed_gqa(q, k, v, q_gain, k_gain, rel_table, sink):
    B, S = q.shape[0], q.shape[1]
    nb = S // ATTN_BLOCK
    q = q.reshape(B, S, N_HEADS_ATTN, HEAD_DIM_ATTN)
    k = k.reshape(B, S, N_KV_HEADS, HEAD_DIM_ATTN)
    v = v.reshape(B, S, N_KV_HEADS, HEAD_DIM_ATTN)
    q = rmsnorm(q, q_gain) * (HEAD_DIM_ATTN ** -0.5)
    k = rmsnorm(k, k_gain)
    qb = q.reshape(B, nb, ATTN_BLOCK, N_KV_HEADS, ATTN_GROUP, HEAD_DIM_ATTN)
    kw = band_windows(k, nb)
    vw = band_windows(v, nb)
    scores = jnp.einsum('bnqkgd,bnskd->bkgnqs', qb, kw).astype(jnp.float32)
    qi = jnp.arange(ATTN_BLOCK)
    ks = jnp.arange(3 * ATTN_BLOCK) - ATTN_BLOCK
    rel = ks[None, :] - qi[:, None]
    bias = jnp.transpose(rel_table[t5_bucket(rel)].astype(jnp.float32), (2, 0, 1))
    bias = bias.reshape(N_KV_HEADS, ATTN_GROUP, 1, ATTN_BLOCK, 3 * ATTN_BLOCK)
    kpos = jnp.arange(nb)[:, None, None] * ATTN_BLOCK + ks[None, None, :]
    valid = (jnp.abs(rel) <= WINDOW)[None] & (kpos >= 0) & (kpos < S)
    scores = jnp.where(valid, scores + bias, NEG_INF)
    sink_col = jnp.broadcast_to(sink.astype(jnp.float32).reshape(N_KV_HEADS, ATTN_GROUP, 1, 1, 1),
                                scores.shape[:-1] + (1,))
    probs = jax.nn.softmax(jnp.concatenate([scores, sink_col], axis=-1), axis=-1)[..., :-1]
    out = jnp.einsum('bkgnqs,bnskd->bnqkgd', probs.astype(v.dtype), vw)
    return out.reshape(B, S, ATTN_Q_W)


def mlstm_scan(q, k, v, ig, lf):
    B, H, S, d = q.shape
    L = MLSTM_CHUNK
    nc = S // L

    def chunks(t):
        return jnp.moveaxis(t.reshape((B, H, nc, L) + t.shape[3:]), 2, 0)

    tril = jnp.tril(jnp.ones((L, L), dtype=bool))

    def step(carry, inp):
        C, n, m = carry
        qc, kc, vc, igc, lfc = inp
        b = jnp.cumsum(lfc, axis=-1)
        dlog = jnp.where(tril, b[..., :, None] - b[..., None, :] + igc[..., None, :], -jnp.inf)
        g_inter = b + m[..., None]
        m_row = jnp.maximum(g_inter, jnp.max(dlog, axis=-1))
        w = jnp.exp(dlog - m_row[..., None])
        s = jnp.einsum('bhtd,bhsd->bhts', qc, kc) * w
        inter = jnp.exp(g_inter - m_row)
        num = jnp.einsum('bhts,bhsd->bhtd', s, vc) + inter[..., None] * jnp.einsum('bhtd,bhde->bhte', qc, C)
        den = jnp.sum(s, axis=-1) + inter * jnp.einsum('bhtd,bhd->bht', qc, n)
        h = num / jnp.maximum(jnp.abs(den), jnp.exp(-m_row))[..., None]
        b_last = b[..., -1]
        wlog = b_last[..., None] - b + igc
        m_new = jnp.maximum(b_last + m, jnp.max(wlog, axis=-1))
        decay = jnp.exp(b_last + m - m_new)
        ws = jnp.exp(wlog - m_new[..., None])
        C_new = decay[..., None, None] * C + jnp.einsum('bhs,bhsd,bhse->bhde', ws, kc, vc)
        n_new = decay[..., None] * n + jnp.einsum('bhs,bhsd->bhd', ws, kc)
        return (C_new, n_new, m_new), h

    init = (jnp.zeros((B, H, d, d), jnp.float32), jnp.zeros((B, H, d), jnp.float32),
            jnp.zeros((B, H), jnp.float32))
    _, h = lax.scan(step, init, (chunks(q), chunks(k), chunks(v), chunks(ig), chunks(lf)))
    return jnp.moveaxis(h, 0, 2).reshape(B, H, S, d)


def mlstm_branch(qm, km, vm, om, gates, conv_w, conv_b, gate_b, out_gain):
    B, S = qm.shape[0], qm.shape[1]
    qk = jnp.concatenate([qm, km], axis=-1)
    qk = lax.conv_general_dilated(qk, conv_w[:, None, :].astype(qk.dtype), (1,),
                                  [(CONV_WIDTH // 2, CONV_WIDTH // 2)],
                                  dimension_numbers=('NWC', 'WIO', 'NWC'),
                                  feature_group_count=qk.shape[-1]) + conv_b
    qk = jax.nn.silu(qk)
    qm, km = qk[..., :MLSTM_W], qk[..., MLSTM_W:]

    def heads(t):
        return t.reshape(B, S, N_HEADS_MLSTM, HEAD_DIM_MLSTM).transpose(0, 2, 1, 3).astype(jnp.float32)

    q = heads(qm)
    k = heads(km) * (HEAD_DIM_MLSTM ** -0.5)
    v = heads(vm)
    g = gates.astype(jnp.float32).reshape(B, S, 2, 2, N_HEADS_MLSTM) + gate_b.astype(jnp.float32)
    g = jnp.transpose(g, (2, 3, 0, 4, 1))
    h_f = mlstm_scan(q, k, v, g[0, 0], jax.nn.log_sigmoid(g[0, 1]))

    def flip(t):
        return jnp.flip(t, axis=2)

    h_b = flip(mlstm_scan(flip(q), flip(k), flip(v), flip(g[1, 0]), flip(jax.nn.log_sigmoid(g[1, 1]))))
    h = (h_f + h_b).transpose(0, 2, 1, 3)
    h = rmsnorm(h, out_gain)
    return h.reshape(B, S, MLSTM_W).astype(om.dtype) * jax.nn.sigmoid(om)


def hier_moe(h, w_rg, b_rg, w_re, b_re, w_gate, w_up, w_down):
    B, S, D = h.shape
    T = B * S
    xt = h.reshape(T, D)
    xf = xt.astype(jnp.float32)
    g_logits = xf @ w_rg.astype(jnp.float32) + b_rg.astype(jnp.float32)
    g_idx = jnp.argmax(g_logits, axis=-1).astype(jnp.int32)
    p_group = jnp.take_along_axis(jax.nn.softmax(g_logits, axis=-1), g_idx[:, None], axis=1)[:, 0]
    e_logits = (xf @ w_re.astype(jnp.float32) + b_re.astype(jnp.float32)).reshape(T, N_GROUPS, EXPERTS_PER_GROUP)
    e_in = jnp.take_along_axis(e_logits, g_idx[:, None, None], axis=1)[:, 0]
    top_v, top_i = lax.top_k(e_in, TOP_K_EXPERT)
    top_w = jax.nn.softmax(top_v, axis=-1) * p_group[:, None]
    expert_ids = (g_idx[:, None] * EXPERTS_PER_GROUP + top_i).reshape(-1).astype(jnp.int32)
    weights = top_w.reshape(-1)
    token_ids = jnp.repeat(jnp.arange(T, dtype=jnp.int32), TOP_K_EXPERT)
    A = T * TOP_K_EXPERT
    order = jnp.argsort(expert_ids)
    se, stok, sw = expert_ids[order], token_ids[order], weights[order]
    counts = jnp.bincount(expert_ids, length=N_EXPERTS)
    starts = jnp.cumsum(counts) - counts
    pcounts = ((counts + MOE_BLOCK - 1) // MOE_BLOCK) * MOE_BLOCK
    pends = jnp.cumsum(pcounts)
    pstarts = pends - pcounts
    dest = pstarts[se] + (jnp.arange(A, dtype=jnp.int32) - starts[se])
    P = A + N_EXPERTS * MOE_BLOCK
    nblk = P // MOE_BLOCK
    row_tok = jnp.zeros((P,), jnp.int32).at[dest].set(stok)
    row_w = jnp.zeros((P,), jnp.float32).at[dest].set(sw)
    blk_e = jnp.minimum(jnp.searchsorted(pends, jnp.arange(nblk) * MOE_BLOCK, side='right'),
                        N_EXPERTS - 1).astype(jnp.int32)
    xs = xt[row_tok].reshape(nblk, MOE_BLOCK, D)

    def expert_block(args):
        xb, e = args
        return (jax.nn.silu(xb @ w_gate[e]) * (xb @ w_up[e])) @ w_down[e]

    ys = lax.map(expert_block, (xs, blk_e)).reshape(P, D)
    y = jnp.zeros((T, D), ys.dtype).at[row_tok].add(ys * row_w[:, None].astype(ys.dtype))
    return y.reshape(B, S, D).astype(h.dtype)


def setup_inputs(seed: int = 0) -> dict:
    key = jax.random.key(seed)
    ks = jax.random.split(key, 24)

    def nrm(k, shape, scale):
        return jax.random.normal(k, shape, jnp.float32) * scale

    ib = nrm(ks[9], (DEPTH, 2, 1, N_HEADS_MLSTM), 0.1)
    fb = jnp.linspace(3.0, 6.0, N_HEADS_MLSTM, dtype=jnp.float32)[None, None, None, :] + nrm(ks[10], (DEPTH, 2, 1, N_HEADS_MLSTM), 0.01)
    return {
        'x': nrm(ks[0], (BATCH, SEQ, D_MODEL), 1.0),
        'norm1_g': 1.0 + nrm(ks[1], (DEPTH, D_MODEL), 0.01),
        'w_in': nrm(ks[2], (DEPTH, D_MODEL, D_IN_PROJ), D_MODEL ** -0.5),
        'attn_q_gain': 1.0 + nrm(ks[3], (DEPTH, HEAD_DIM_ATTN), 0.01),
        'attn_k_gain': 1.0 + nrm(ks[4], (DEPTH, HEAD_DIM_ATTN), 0.01),
        'attn_sink': nrm(ks[5], (DEPTH, N_HEADS_ATTN), 0.5),
        'rel_bias_table': nrm(ks[6], (N_REL_BUCKETS, N_HEADS_ATTN), 0.5),
        'mlstm_conv_w': nrm(ks[7], (DEPTH, CONV_WIDTH, 2 * MLSTM_W), CONV_WIDTH ** -0.5),
        'mlstm_conv_b': nrm(ks[8], (DEPTH, 2 * MLSTM_W), 0.01),
        'mlstm_gate_b': jnp.concatenate([ib, fb], axis=2),
        'mlstm_out_gain': 1.0 + nrm(ks[11], (DEPTH, N_HEADS_MLSTM, HEAD_DIM_MLSTM), 0.01),
        'w_branch_attn': nrm(ks[12], (DEPTH, ATTN_Q_W, D_MODEL), ATTN_Q_W ** -0.5),
        'w_branch_mlstm': nrm(ks[13], (DEPTH, MLSTM_W, D_MODEL), MLSTM_W ** -0.5),
        'merge_b': nrm(ks[14], (DEPTH, MERGE_W), 0.01),
        'w_out': nrm(ks[15], (DEPTH, D_MODEL, D_MODEL), D_MODEL ** -0.5),
        'norm2_g': 1.0 + nrm(ks[16], (DEPTH, D_MODEL), 0.01),
        'w_router_group': nrm(ks[17], (DEPTH, D_MODEL, N_GROUPS), D_MODEL ** -0.5),
        'b_router_group': nrm(ks[18], (DEPTH, N_GROUPS), 0.01),
        'w_router_expert': nrm(ks[19], (DEPTH, D_MODEL, N_EXPERTS), D_MODEL ** -0.5),
        'b_router_expert': nrm(ks[20], (DEPTH, N_EXPERTS), 0.01),
        'w_expert_gate': nrm(ks[21], (DEPTH, N_EXPERTS, D_MODEL, D_EXPERT), D_MODEL ** -0.5),
        'w_expert_up': nrm(ks[22], (DEPTH, N_EXPERTS, D_MODEL, D_EXPERT), D_MODEL ** -0.5),
        'w_expert_down': nrm(ks[23], (DEPTH, N_EXPERTS, D_EXPERT, D_MODEL), D_EXPERT ** -0.5),
    }


def reference(x, norm1_g, w_in, attn_q_gain, attn_k_gain, attn_sink, rel_bias_table,
              mlstm_conv_w, mlstm_conv_b, mlstm_gate_b, mlstm_out_gain,
              w_branch_attn, w_branch_mlstm, merge_b, w_out, norm2_g,
              w_router_group, b_router_group, w_router_expert, b_router_expert,
              w_expert_gate, w_expert_up, w_expert_down):
    for l in range(DEPTH):
        h = rmsnorm(x, norm1_g[l])
        proj = h @ w_in[l]
        qa, ka, va, qm, km, vm, om, gm, mg = split_cols(proj, IN_SPLITS)
        ya = windowed_gqa(qa, ka, va, attn_q_gain[l], attn_k_gain[l], rel_bias_table, attn_sink[l])
        ym = mlstm_branch(qm, km, vm, om, gm, mlstm_conv_w[l], mlstm_conv_b[l],
                          mlstm_gate_b[l], mlstm_out_gain[l])
        gates = jax.nn.sigmoid(mg + merge_b[l])
        u = gates[..., :D_MODEL] * (ya @ w_branch_attn[l]) + gates[..., D_MODEL:] * (ym @ w_branch_mlstm[l])
        x = x + u @ w_out[l]
        x = x + hier_moe(rmsnorm(x, norm2_g[l]), w_router_group[l], b_router_group[l],
                         w_router_expert[l], b_router_expert[l],
                         w_expert_gate[l], w_expert_up[l], w_expert_down[l])
    return x
```

```python
import functools
import math

import jax
import jax.numpy as jnp
from jax import lax
from jax.experimental import pallas as pl
from jax.experimental.pallas import tpu as pltpu

F32 = jnp.float32
BF16 = jnp.bfloat16
I32 = jnp.int32

LANES = 128
SUBLANES = 8
VMEM_LIMIT_BYTES = 56 * 1024 * 1024

D_MODEL = 1024
N_HEADS_ATTN = 8
N_KV_HEADS = 2
HEAD_DIM_ATTN = 64
WINDOW = 128
ATTN_BLOCK = WINDOW
N_REL_BUCKETS = 32
REL_MAX_DISTANCE = 128
N_HEADS_MLSTM = 4
HEAD_DIM_MLSTM = 128
CONV_WIDTH = 5
ATTN_Q_W = N_HEADS_ATTN * HEAD_DIM_ATTN
ATTN_KV_W = N_KV_HEADS * HEAD_DIM_ATTN
MLSTM_W = N_HEADS_MLSTM * HEAD_DIM_MLSTM
MLSTM_GATE_W = 2 * 2 * N_HEADS_MLSTM
N_GROUPS = 4
EXPERTS_PER_GROUP = 8
N_EXPERTS = N_GROUPS * EXPERTS_PER_GROUP
TOP_K = 2
D_EXPERT = 512
RMS_EPS = 1e-6
NEG_INF = -1e30

_OFF_QA = 0
_OFF_VM = ATTN_Q_W + 2 * ATTN_KV_W + 2 * MLSTM_W
_OFF_OM = _OFF_VM + MLSTM_W
_OFF_GM = _OFF_OM + MLSTM_W
_OFF_MG = _OFF_GM + MLSTM_GATE_W
_W_IN = _OFF_MG + 2 * D_MODEL

P_MG = 0
P_QA = 2 * D_MODEL
P_KA = P_QA + ATTN_Q_W
P_VA = P_KA + ATTN_KV_W
P_QM = P_VA + ATTN_KV_W
P_KM = P_QM + MLSTM_W
P_OM = P_KM + MLSTM_W
P_W = P_OM + MLSTM_W

MLSTM_L = 128
TOK_TILE = 512
MOE_BLK = 256
ROW_TILE = SUBLANES

_NT = (((1,), (1,)), ((), ()))


def _dot(a, b):
    return jnp.dot(a, b, preferred_element_type=F32)


def _dot_nt(a, b):
    return lax.dot_general(a, b, _NT, preferred_element_type=F32)


def _inproj_kernel(x_ref, g1_ref, wm_ref, wvt_ref, wgt_ref, gb_ref, p_ref, vt_ref, gt_ref):
    x = x_ref[...]
    h = x * lax.rsqrt(jnp.mean(x * x, axis=-1, keepdims=True) + RMS_EPS) * g1_ref[...]
    hb = h.astype(BF16)
    step = 512
    for c0 in range(0, P_W, step):
        w = min(step, P_W - c0)
        p_ref[:, c0:c0 + w] = _dot(hb, wm_ref[:, c0:c0 + w]).astype(BF16)
    vt = _dot_nt(wvt_ref[...], hb)
    gt = _dot_nt(wgt_ref[...], hb) + gb_ref[...]
    for cc in range(x.shape[0] // LANES):
        vt_ref[cc] = vt[:, cc * LANES:(cc + 1) * LANES].astype(BF16)
        gt_ref[cc] = gt[:, cc * LANES:(cc + 1) * LANES]


def _in_proj(x2, g1, w_main, w_vt, w_gt, gate_b):
    T = x2.shape[0]
    tm = TOK_TILE
    nck = tm // LANES
    return pl.pallas_call(
        _inproj_kernel,
        grid=(T // tm,),
        in_specs=[
            pl.BlockSpec((tm, D_MODEL), lambda i: (i, 0)),
            pl.BlockSpec((1, D_MODEL), lambda i: (0, 0)),
            pl.BlockSpec((D_MODEL, P_W), lambda i: (0, 0)),
            pl.BlockSpec((MLSTM_W, D_MODEL), lambda i: (0, 0)),
            pl.BlockSpec((MLSTM_GATE_W, D_MODEL), lambda i: (0, 0)),
            pl.BlockSpec((MLSTM_GATE_W, 1), lambda i: (0, 0)),
        ],
        out_specs=[
            pl.BlockSpec((tm, P_W), lambda i: (i, 0)),
            pl.BlockSpec((nck, MLSTM_W, LANES), lambda i: (i, 0, 0)),
            pl.BlockSpec((nck, MLSTM_GATE_W, LANES), lambda i: (i, 0, 0)),
        ],
        out_shape=[
            jax.ShapeDtypeStruct((T, P_W), BF16),
            jax.ShapeDtypeStruct((T // LANES, MLSTM_W, LANES), BF16),
            jax.ShapeDtypeStruct((T // LANES, MLSTM_GATE_W, LANES), F32),
        ],
        compiler_params=pltpu.CompilerParams(
            dimension_semantics=("arbitrary",), vmem_limit_bytes=VMEM_LIMIT_BYTES),
        name="in_proj",
    )(x2, g1, w_main, w_vt, w_gt, gate_b)


def _attn_kernel(sink_ref, q_ref, kp_ref, kc_ref, kn_ref, vp_ref, vc_ref, vn_ref,
                 bias_ref, qg_ref, kg_ref, o_ref, *, nb):
    n = pl.program_id(1)
    half = HEAD_DIM_ATTN
    r_i = lax.broadcasted_iota(I32, (LANES, LANES), 0)
    c_i = lax.broadcasted_iota(I32, (LANES, LANES), 1)
    bd = ((r_i // half) == (c_i // half)).astype(BF16)

    def headnorm(t, gain):
        tt = t * t
        hi = tt.astype(BF16)
        lo = (tt - hi.astype(F32)).astype(BF16)
        ms = (_dot(hi, bd) + _dot(lo, bd)) * (1.0 / half)
        return t * lax.rsqrt(ms + RMS_EPS) * gain

    lane = lax.broadcasted_iota(I32, (1, LANES), 1)
    lo_mask = lane < half

    k = jnp.concatenate([kp_ref[...], kc_ref[...], kn_ref[...]], axis=0).astype(F32)
    kn = headnorm(k, kg_ref[...])
    k_nat = kn.astype(BF16)
    k_rot = pltpu.roll(kn, half, axis=1).astype(BF16)
    v = jnp.concatenate([vp_ref[...], vc_ref[...], vn_ref[...]], axis=0).astype(F32)
    v_rot = pltpu.roll(v, half, axis=1)
    zero = jnp.zeros_like(v)
    v_lo = [jnp.where(lo_mask, v, zero).astype(BF16), jnp.where(lo_mask, v_rot, zero).astype(BF16)]
    v_hi = [jnp.where(lo_mask, zero, v_rot).astype(BF16), jnp.where(lo_mask, zero, v).astype(BF16)]

    col = lax.broadcasted_iota(I32, (1, 3 * ATTN_BLOCK), 1)
    col_lo = jnp.where(n > 0, 0, ATTN_BLOCK)
    col_hi = jnp.where(n < nb - 1, 3 * ATTN_BLOCK, 2 * ATTN_BLOCK)
    col_ok = (col >= col_lo) & (col < col_hi)

    for j in range(ATTN_Q_W // LANES):
        qt = q_ref[:, j * LANES:(j + 1) * LANES].astype(F32)
        qn = headnorm(qt, qg_ref[...]) * (HEAD_DIM_ATTN ** -0.5)
        acc = None
        for p in range(2):
            hd = 2 * j + p
            kv = hd // (N_HEADS_ATTN // N_KV_HEADS)
            qm = jnp.where(lo_mask if p == 0 else ~lo_mask, qn, 0.0).astype(BF16)
            kx = k_nat if (kv == p) else k_rot
            s = _dot_nt(qm, kx) + bias_ref[hd]
            s = jnp.where(col_ok, s, NEG_INF)
            sink = sink_ref[hd]
            m = jnp.maximum(jnp.max(s, axis=1, keepdims=True), sink)
            e = jnp.exp(s - m)
            den = jnp.sum(e, axis=1, keepdims=True) + jnp.exp(sink - m)
            probs = (e * (1.0 / den)).astype(BF16)
            vx = v_lo[kv] if p == 0 else v_hi[kv]
            o = _dot(probs, vx)
            acc = o if acc is None else acc + o
        o_ref[:, j * LANES:(j + 1) * LANES] = acc.astype(BF16)


def _attention(P, bias, sink, qg, kg, B, S):
    nb = S // ATTN_BLOCK
    Q = ATTN_BLOCK
    qcol = P_QA // ATTN_Q_W
    kcol = P_KA // ATTN_KV_W
    vcol = P_VA // ATTN_KV_W

    def blk(col, shift):
        def imap(b, n, sink_ref):
            nn = jnp.clip(n + shift, 0, nb - 1)
            return (b * nb + nn, col)
        return pl.BlockSpec((Q, ATTN_KV_W), imap)

    grid_spec = pltpu.PrefetchScalarGridSpec(
        num_scalar_prefetch=1,
        grid=(B, nb),
        in_specs=[
            pl.BlockSpec((Q, ATTN_Q_W), lambda b, n, s: (b * nb + n, qcol)),
            blk(kcol, -1), blk(kcol, 0), blk(kcol, 1),
            blk(vcol, -1), blk(vcol, 0), blk(vcol, 1),
            pl.BlockSpec((N_HEADS_ATTN, Q, 3 * Q), lambda b, n, s: (0, 0, 0)),
            pl.BlockSpec((1, LANES), lambda b, n, s: (0, 0)),
            pl.BlockSpec((1, LANES), lambda b, n, s: (0, 0)),
        ],
        out_specs=pl.BlockSpec((Q, ATTN_Q_W), lambda b, n, s: (b * nb + n, 0)),
    )
    return pl.pallas_call(
        functools.partial(_attn_kernel, nb=nb),
        grid_spec=grid_spec,
        out_shape=jax.ShapeDtypeStruct((B * S, ATTN_Q_W), BF16),
        compiler_params=pltpu.CompilerParams(
            dimension_semantics=("arbitrary", "arbitrary"), vmem_limit_bytes=VMEM_LIMIT_BYTES),
        name="attn",
    )(sink, P, P, P, P, P, P, P, bias, qg, kg)


def _log_sigmoid(x):
    return jnp.minimum(x, 0.0) - jnp.log1p(jnp.exp(-jnp.abs(x)))


def _mlstm_kernel(q_ref, k_ref, vt_ref, o_ref, gt_ref, cwq_ref, cwk_ref, cbq_ref, cbk_ref, og_ref,
                  ym_ref,
                  pad_ref, qs_ref, ks_ref, v1t_ref, ls_ref, cs_ref, ct_ref, m_ref, hs_ref):
    L = MLSTM_L
    S = q_ref.shape[0]
    nc = S // L
    hh = pl.program_id(1)
    H = N_HEADS_MLSTM

    halo = SUBLANES
    zpad = jnp.zeros((halo, LANES), F32)
    pad_ref[0:halo, :] = zpad
    pad_ref[halo + S:2 * halo + S, :] = zpad

    def conv_silu(src_ref, w_ref, b_ref, dst_ref, scale):
        pad_ref[halo:halo + S, :] = src_ref[...].astype(F32)
        for c in range(nc):
            acc = jnp.broadcast_to(b_ref[...], (L, LANES))
            for j in range(CONV_WIDTH):
                off = c * L + halo - CONV_WIDTH // 2 + j
                acc = acc + pad_ref[off:off + L, :] * w_ref[j:j + 1, :]
            y = acc * jax.nn.sigmoid(acc)
            dst_ref[c * L:(c + 1) * L, :] = (y * scale).astype(BF16)

    conv_silu(q_ref, cwq_ref, cbq_ref, qs_ref, 1.0)
    conv_silu(k_ref, cwk_ref, cbk_ref, ks_ref, HEAD_DIM_MLSTM ** -0.5)

    ones_blk = jnp.ones((L, LANES), BF16)
    for c in range(nc):
        v1t_ref[c, 0:L, :] = vt_ref[c]
        v1t_ref[c, L:2 * L, :] = ones_blk

    g_all = gt_ref[...].reshape(nc * MLSTM_GATE_W, LANES)
    ls = _log_sigmoid(g_all)
    lane = lax.broadcasted_iota(I32, (1, LANES), 1)
    pre = ls
    suf = ls
    d = 1
    while d < LANES:
        pre = pre + jnp.where(lane >= d, pltpu.roll(pre, d, axis=1), 0.0)
        suf = suf + jnp.where(lane < LANES - d, pltpu.roll(suf, LANES - d, axis=1), 0.0)
        d *= 2
    ls_ref[...] = ls.reshape(nc, MLSTM_GATE_W, LANES)
    cs_ref[0] = pre.reshape(nc, MLSTM_GATE_W, LANES)
    cs_ref[1] = suf.reshape(nc, MLSTM_GATE_W, LANES)

    ct_ref[...] = jnp.zeros(ct_ref.shape, F32)
    m_ref[...] = jnp.zeros(m_ref.shape, F32)

    r_i = lax.broadcasted_iota(I32, (L, L), 0)
    c_i = lax.broadcasted_iota(I32, (L, L), 1)
    masks = (c_i <= r_i, c_i >= r_i)

    def chunk_step(dr, c, accumulate):
        mask = masks[dr]
        row_i = dr * 2 * H + hh
        row_f = dr * 2 * H + H + hh
        ig_row = gt_ref[c, pl.ds(row_i, 1), :]
        lf_row = ls_ref[c, pl.ds(row_f, 1), :]
        a_row = ig_row - cs_ref[dr, c, pl.ds(row_f, 1), :]
        am = jnp.where(mask, a_row, -jnp.inf)
        cmax = jnp.max(am, axis=1, keepdims=True)
        bcol = jnp.sum(jnp.where(mask, lf_row, 0.0), axis=1, keepdims=True)
        m_row = m_ref[dr, 0:1, :]
        mm = jnp.maximum(cmax, m_row)
        w = jnp.exp(am - mm)
        inter = jnp.exp(m_row - mm)
        clamp = jnp.exp(-(bcol + mm))
        row0 = pl.multiple_of(c * L, L)
        q_c = qs_ref[pl.ds(row0, L), :]
        k_c = ks_ref[pl.ds(row0, L), :]
        sw = (_dot_nt(q_c, k_c) * w).astype(BF16)
        qi = (q_c.astype(F32) * inter).astype(BF16)
        v1t = v1t_ref[c]
        ct = ct_ref[dr]
        lhs = jnp.concatenate([sw, qi], axis=1)
        rhs_t = jnp.concatenate([v1t, ct.astype(BF16)], axis=1)
        nd = _dot_nt(lhs, rhs_t)
        hval = nd[:, 0:L] / jnp.maximum(jnp.abs(nd[:, L:2 * L]), clamp)
        if accumulate:
            hs_ref[pl.ds(row0, L), :] = hs_ref[pl.ds(row0, L), :] + hval
        else:
            hs_ref[pl.ds(row0, L), :] = hval
        m_last = jnp.maximum(jnp.max(a_row, axis=1, keepdims=True), m_row)
        decay = jnp.exp(m_row - m_last)
        ws_row = jnp.exp(a_row - m_last)
        ut = _dot((v1t.astype(F32) * ws_row).astype(BF16), k_c)
        ct_ref[dr] = decay * ct + ut
        m_new = jnp.sum(lf_row, axis=1, keepdims=True) + m_last
        m_ref[dr] = jnp.broadcast_to(m_new, (SUBLANES, LANES))

    def first_half(i, carry):
        chunk_step(0, i, False)
        chunk_step(1, nc - 1 - i, False)
        return carry

    def second_half(i, carry):
        chunk_step(0, i, True)
        chunk_step(1, nc - 1 - i, True)
        return carry

    lax.fori_loop(0, nc // 2, first_half, 0)
    lax.fori_loop(nc // 2, nc, second_half, 0)

    for c in range(nc):
        hsum = hs_ref[c * L:(c + 1) * L, :]
        y = hsum * lax.rsqrt(jnp.mean(hsum * hsum, axis=-1, keepdims=True) + RMS_EPS) * og_ref[0]
        og = jax.nn.sigmoid(o_ref[c * L:(c + 1) * L, :].astype(F32))
        ym_ref[c * L:(c + 1) * L, :] = (y * og).astype(BF16)


def _mlstm(P, VT, GT, conv_w, conv_b, out_gain, B, S):
    H = N_HEADS_MLSTM
    L = MLSTM_L
    nc = S // L
    qcol = P_QM // LANES
    kcol = P_KM // LANES
    ocol = P_OM // LANES
    return pl.pallas_call(
        _mlstm_kernel,
        grid=(B, H),
        in_specs=[
            pl.BlockSpec((S, LANES), lambda b, h: (b, qcol + h)),
            pl.BlockSpec((S, LANES), lambda b, h: (b, kcol + h)),
            pl.BlockSpec((nc, LANES, LANES), lambda b, h: (b, h, 0)),
            pl.BlockSpec((S, LANES), lambda b, h: (b, ocol + h)),
            pl.BlockSpec((nc, MLSTM_GATE_W, LANES), lambda b, h: (b, 0, 0)),
            pl.BlockSpec((CONV_WIDTH, LANES), lambda b, h: (0, h)),
            pl.BlockSpec((CONV_WIDTH, LANES), lambda b, h: (0, H + h)),
            pl.BlockSpec((1, LANES), lambda b, h: (0, h)),
            pl.BlockSpec((1, LANES), lambda b, h: (0, H + h)),
            pl.BlockSpec((1, 1, LANES), lambda b, h: (h, 0, 0)),
        ],
        out_specs=pl.BlockSpec((S, LANES), lambda b, h: (b, h)),
        out_shape=jax.ShapeDtypeStruct((B * S, MLSTM_W), BF16),
        scratch_shapes=[
            pltpu.VMEM((S + 2 * SUBLANES, LANES), F32),
            pltpu.VMEM((S, LANES), BF16),
            pltpu.VMEM((S, LANES), BF16),
            pltpu.VMEM((nc, 2 * L, LANES), BF16),
            pltpu.VMEM((nc, MLSTM_GATE_W, LANES), F32),
            pltpu.VMEM((2, nc, MLSTM_GATE_W, LANES), F32),
            pltpu.VMEM((2, 2 * L, LANES), F32),
            pltpu.VMEM((2, SUBLANES, LANES), F32),
            pltpu.VMEM((S, LANES), F32),
        ],
        compiler_params=pltpu.CompilerParams(
            dimension_semantics=("arbitrary", "arbitrary"), vmem_limit_bytes=VMEM_LIMIT_BYTES),
        name="mlstm",
    )(P, P, VT, P, GT, conv_w, conv_w, conv_b, conv_b, out_gain[:, None, :])


def _merge_kernel(ya_ref, ym_ref, mga_ref, mgm_ref, mb_ref, x_ref, wa_ref, wm_ref, wo_ref, g2_ref,
                  wrh_ref, wrl_ref, br_ref, x1_ref, h2_ref, eid_ref, wt_ref):
    tm = x_ref.shape[0]
    a = _dot(ya_ref[...], wa_ref[...])
    mm = _dot(ym_ref[...], wm_ref[...])
    ga = jax.nn.sigmoid(mga_ref[...].astype(F32) + mb_ref[:, 0:D_MODEL])
    gm = jax.nn.sigmoid(mgm_ref[...].astype(F32) + mb_ref[:, D_MODEL:2 * D_MODEL])
    u = (ga * a + gm * mm).astype(BF16)
    x1 = x_ref[...] + _dot(u, wo_ref[...])
    x1_ref[...] = x1
    h2 = x1 * lax.rsqrt(jnp.mean(x1 * x1, axis=-1, keepdims=True) + RMS_EPS) * g2_ref[...]
    for j in range(D_MODEL // LANES):
        h2_ref[pl.ds(j, tm, stride=ROW_TILE), :] = h2[:, j * LANES:(j + 1) * LANES]

    hi = h2.astype(BF16)
    lo = (h2 - hi.astype(F32)).astype(BF16)
    lg = (_dot_nt(wrh_ref[...], hi) + _dot_nt(wrh_ref[...], lo) + _dot_nt(wrl_ref[...], hi)
          + br_ref[...])
    row = lax.broadcasted_iota(I32, (SUBLANES, tm), 0)
    gl = jnp.where(row < N_GROUPS, lg[0:SUBLANES], -jnp.inf)
    gmax = jnp.max(gl, axis=0, keepdims=True)
    g_idx = jnp.min(jnp.where(gl == gmax, row, SUBLANES), axis=0, keepdims=True)
    p_group = 1.0 / jnp.sum(jnp.exp(gl - gmax), axis=0, keepdims=True)
    e_in = lg[SUBLANES * N_GROUPS:SUBLANES * (N_GROUPS + 1)]
    for g in range(N_GROUPS - 2, -1, -1):
        e_in = jnp.where(g_idx == g, lg[SUBLANES * (g + 1):SUBLANES * (g + 2)], e_in)
    v1 = jnp.max(e_in, axis=0, keepdims=True)
    i1 = jnp.min(jnp.where(e_in == v1, row, SUBLANES), axis=0, keepdims=True)
    e2 = jnp.where(row == i1, -jnp.inf, e_in)
    v2 = jnp.max(e2, axis=0, keepdims=True)
    i2 = jnp.min(jnp.where(e2 == v2, row, SUBLANES), axis=0, keepdims=True)
    t = jnp.exp(v2 - v1)
    w1 = p_group / (1.0 + t)
    w2 = p_group * t / (1.0 + t)
    id1 = g_idx * EXPERTS_PER_GROUP + i1
    id2 = g_idx * EXPERTS_PER_GROUP + i2
    eid_ref[...] = jnp.where(row == 0, id1, jnp.where(row == 1, id2, 0))
    wt_ref[...] = jnp.where(row == 0, w1, jnp.where(row == 1, w2, 0.0))


def _merge(ya, ym, P, merge_b, x2, w_a, w_m, w_o, g2, wr_hi, wr_lo, b_r):
    T = x2.shape[0]
    tm = TOK_TILE
    const = lambda i: (0, 0)
    return pl.pallas_call(
        _merge_kernel,
        grid=(T // tm,),
        in_specs=[
            pl.BlockSpec((tm, ATTN_Q_W), lambda i: (i, 0)),
            pl.BlockSpec((tm, MLSTM_W), lambda i: (i, 0)),
            pl.BlockSpec((tm, D_MODEL), lambda i: (i, 0)),
            pl.BlockSpec((tm, D_MODEL), lambda i: (i, 1)),
            pl.BlockSpec((1, 2 * D_MODEL), const),
            pl.BlockSpec((tm, D_MODEL), lambda i: (i, 0)),
            pl.BlockSpec((ATTN_Q_W, D_MODEL), const),
            pl.BlockSpec((MLSTM_W, D_MODEL), const),
            pl.BlockSpec((D_MODEL, D_MODEL), const),
            pl.BlockSpec((1, D_MODEL), const),
            pl.BlockSpec((LANES, D_MODEL), const),
            pl.BlockSpec((LANES, D_MODEL), const),
            pl.BlockSpec((LANES, 1), const),
        ],
        out_specs=[
            pl.BlockSpec((tm, D_MODEL), lambda i: (i, 0)),
            pl.BlockSpec((tm * ROW_TILE, LANES), lambda i: (i, 0)),
            pl.BlockSpec((SUBLANES, tm), lambda i: (0, i)),
            pl.BlockSpec((SUBLANES, tm), lambda i: (0, i)),
        ],
        out_shape=[
            jax.ShapeDtypeStruct((T, D_MODEL), F32),
            jax.ShapeDtypeStruct((T * ROW_TILE, LANES), F32),
            jax.ShapeDtypeStruct((SUBLANES, T), I32),
            jax.ShapeDtypeStruct((SUBLANES, T), F32),
        ],
        compiler_params=pltpu.CompilerParams(
            dimension_semantics=("arbitrary",), vmem_limit_bytes=VMEM_LIMIT_BYTES),
        name="merge",
    )(ya, ym, P, P, merge_b, x2, w_a, w_m, w_o, g2, wr_hi, wr_lo, b_r)


def _moe_kernel(blk_e_ref, nval_ref, nused_ref,
                tok_ref, tokn_ref, dst_ref, h2_hbm, roww_ref, wg_ref, wu_ref, wd_ref,
                y_hbm,
                xg_ref, ys_ref, xs_ref, wgb_ref, wub_ref, wdb_ref, gsem, ssem):
    blk = MOE_BLK
    b = pl.program_id(0)
    nused = nused_ref[0]
    slot = lax.rem(b, 2)

    def gather_copy(r, tok, slot_):
        return pltpu.make_async_copy(
            h2_hbm.at[pl.ds(tok * ROW_TILE, ROW_TILE), :],
            xg_ref.at[slot_, pl.ds(r * ROW_TILE, ROW_TILE), :],
            gsem.at[slot_])

    def scatter_copy(r, dst, slot_):
        return pltpu.make_async_copy(
            ys_ref.at[slot_, pl.ds(r * ROW_TILE, ROW_TILE), :],
            y_hbm.at[pl.ds(dst * ROW_TILE, ROW_TILE), :],
            ssem.at[slot_])

    def gather_start(idx_ref, slot_):
        def body(r, carry):
            gather_copy(r, idx_ref[0, 0, r], slot_).start()
            return carry
        lax.fori_loop(0, blk, body, 0, unroll=8)

    def gather_wait(slot_):
        def body(r, carry):
            gather_copy(r, 0, slot_).wait()
            return carry
        lax.fori_loop(0, blk, body, 0, unroll=8)

    def scatter_start(slot_, nrows):
        def body(r, carry):
            scatter_copy(r, dst_ref[0, 0, r], slot_).start()
            return carry
        lax.fori_loop(0, nrows, body, 0)

    def scatter_wait(slot_, nrows):
        def body(r, carry):
            scatter_copy(r, 0, slot_).wait()
            return carry
        lax.fori_loop(0, nrows, body, 0)

    @pl.when(b < nused)
    def _():
        @pl.when(b == 0)
        def _():
            gather_start(tok_ref, 0)

        gather_wait(slot)

        @pl.when(b + 1 < nused)
        def _():
            gather_start(tokn_ref, 1 - slot)

        changed = (b == 0) | (blk_e_ref[b] != blk_e_ref[jnp.maximum(b - 1, 0)])

        @pl.when(changed)
        def _():
            wgb_ref[...] = wg_ref[0].astype(BF16)
            wub_ref[...] = wu_ref[0].astype(BF16)
            wdb_ref[...] = wd_ref[0].astype(BF16)

        for j in range(D_MODEL // LANES):
            xs_ref[:, j * LANES:(j + 1) * LANES] = (
                xg_ref[slot, pl.ds(j, blk, stride=ROW_TILE), :].astype(BF16))
        x = xs_ref[...]
        g = _dot(x, wgb_ref[...])
        u = _dot(x, wub_ref[...])
        act = (g * jax.nn.sigmoid(g) * u).astype(BF16)
        y = _dot(act, wdb_ref[...]) * roww_ref[...]

        @pl.when(b >= 2)
        def _():
            scatter_wait(slot, nval_ref[jnp.maximum(b - 2, 0)])

        for j in range(D_MODEL // LANES):
            ys_ref[slot, pl.ds(j, blk, stride=ROW_TILE), :] = y[:, j * LANES:(j + 1) * LANES]
        scatter_start(slot, nval_ref[b])

        @pl.when(b == nused - 1)
        def _():
            scatter_wait(slot, nval_ref[b])

            @pl.when(b >= 1)
            def _():
                scatter_wait(1 - slot, nval_ref[jnp.maximum(b - 1, 0)])


def _moe(blk_e, nval, row_tok, row_dst, nused, h2g, row_w, w_gate, w_up, w_down, T):
    blk = MOE_BLK
    nblk = row_tok.shape[0] // blk
    tok3 = row_tok.reshape(nblk, 1, blk)
    dst3 = row_dst.reshape(nblk, 1, blk)
    grid_spec = pltpu.PrefetchScalarGridSpec(
        num_scalar_prefetch=3,
        grid=(nblk,),
        in_specs=[
            pl.BlockSpec((1, 1, blk), lambda b, e, v, n: (b, 0, 0), memory_space=pltpu.SMEM),
            pl.BlockSpec((1, 1, blk), lambda b, e, v, n: (jnp.minimum(b + 1, nblk - 1), 0, 0),
                         memory_space=pltpu.SMEM),
            pl.BlockSpec((1, 1, blk), lambda b, e, v, n: (b, 0, 0), memory_space=pltpu.SMEM),
            pl.BlockSpec(memory_space=pl.ANY),
            pl.BlockSpec((blk, 1), lambda b, e, v, n: (b, 0)),
            pl.BlockSpec((1, D_MODEL, D_EXPERT), lambda b, e, v, n: (e[b], 0, 0)),
            pl.BlockSpec((1, D_MODEL, D_EXPERT), lambda b, e, v, n: (e[b], 0, 0)),
            pl.BlockSpec((1, D_EXPERT, D_MODEL), lambda b, e, v, n: (e[b], 0, 0)),
        ],
        out_specs=pl.BlockSpec(memory_space=pl.ANY),
        scratch_shapes=[
            pltpu.VMEM((2, blk * ROW_TILE, LANES), F32),
            pltpu.VMEM((2, blk * ROW_TILE, LANES), F32),
            pltpu.VMEM((blk, D_MODEL), BF16),
            pltpu.VMEM((D_MODEL, D_EXPERT), BF16),
            pltpu.VMEM((D_MODEL, D_EXPERT), BF16),
            pltpu.VMEM((D_EXPERT, D_MODEL), BF16),
            pltpu.SemaphoreType.DMA((2,)),
            pltpu.SemaphoreType.DMA((2,)),
        ],
    )
    return pl.pallas_call(
        _moe_kernel,
        grid_spec=grid_spec,
        out_shape=jax.ShapeDtypeStruct((TOP_K * T * ROW_TILE, LANES), F32),
        compiler_params=pltpu.CompilerParams(
            dimension_semantics=("arbitrary",), vmem_limit_bytes=VMEM_LIMIT_BYTES),
        name="moe",
    )(blk_e, nval, nused, tok3, tok3, dst3, h2g, row_w, w_gate, w_up, w_down)


def _combine_kernel(x1_ref, y0_ref, y1_ref, o_ref):
    tm = x1_ref.shape[0]
    for j in range(D_MODEL // LANES):
        sl = slice(j * LANES, (j + 1) * LANES)
        o_ref[:, sl] = (x1_ref[:, sl] + y0_ref[pl.ds(j, tm, stride=ROW_TILE), :]
                        + y1_ref[pl.ds(j, tm, stride=ROW_TILE), :])


def _combine(x1, Y):
    T = x1.shape[0]
    tm = TOK_TILE
    nt = T // tm
    return pl.pallas_call(
        _combine_kernel,
        grid=(nt,),
        in_specs=[
            pl.BlockSpec((tm, D_MODEL), lambda i: (i, 0)),
            pl.BlockSpec((tm * ROW_TILE, LANES), lambda i: (i, 0)),
            pl.BlockSpec((tm * ROW_TILE, LANES), lambda i: (i + nt, 0)),
        ],
        out_specs=pl.BlockSpec((tm, D_MODEL), lambda i: (i, 0)),
        out_shape=jax.ShapeDtypeStruct((T, D_MODEL), F32),
        compiler_params=pltpu.CompilerParams(
            dimension_semantics=("arbitrary",), vmem_limit_bytes=VMEM_LIMIT_BYTES),
        name="combine",
    )(x1, Y, Y)


def _t5_bucket(rel):
    half = N_REL_BUCKETS // 2
    max_exact = half // 2
    sign = jnp.where(rel > 0, half, 0)
    n = jnp.abs(rel)
    nf = jnp.maximum(n, 1).astype(jnp.float32)
    large = max_exact + (jnp.log(nf / max_exact) / math.log(REL_MAX_DISTANCE / max_exact)
                         * (half - max_exact)).astype(jnp.int32)
    large = jnp.minimum(large, half - 1)
    return sign + jnp.where(n < max_exact, n, large)


def _attn_bias(rel_table):
    qi = jnp.arange(ATTN_BLOCK)
    ks = jnp.arange(3 * ATTN_BLOCK) - ATTN_BLOCK
    rel = ks[None, :] - qi[:, None]
    bias = jnp.transpose(rel_table[_t5_bucket(rel)].astype(F32), (2, 0, 1))
    return jnp.where((jnp.abs(rel) <= WINDOW)[None], bias, NEG_INF)


def _dispatch_plan(eid, wts, T):
    blk = MOE_BLK
    A = TOP_K * T
    P = A + N_EXPERTS * blk
    nblk = P // blk
    eid_flat = eid.reshape(A)
    w_flat = wts.reshape(A)
    order = jnp.argsort(eid_flat).astype(I32)
    counts = jnp.sum((eid_flat[None, :] == jnp.arange(N_EXPERTS, dtype=I32)[:, None]).astype(I32), axis=1)
    starts = jnp.cumsum(counts) - counts
    pcounts = ((counts + blk - 1) // blk) * blk
    pends = jnp.cumsum(pcounts)
    pstarts = pends - pcounts
    nused = (pends[-1] // blk).astype(I32)
    blk_e = jnp.minimum(jnp.searchsorted(pends, jnp.arange(nblk, dtype=I32) * blk, side='right'),
                        N_EXPERTS - 1).astype(I32)
    last_e = blk_e[jnp.maximum(nused - 1, 0)]
    blk_e = jnp.where(jnp.arange(nblk) < nused, blk_e, last_e)
    r = jnp.arange(P, dtype=I32)
    e_r = blk_e[r // blk]
    off = r - pstarts[e_r]
    valid = (off < counts[e_r]) & (r < nused * blk)
    src = order[jnp.clip(starts[e_r] + off, 0, A - 1)]
    row_tok = jnp.where(valid, src % T, 0).astype(I32)
    row_dst = jnp.where(valid, src, 0).astype(I32)
    row_w = jnp.where(valid, w_flat[src], 0.0).astype(F32)
    nval = jnp.sum(valid.reshape(nblk, blk).astype(I32), axis=1)
    return blk_e, nval, row_tok, row_dst, nused.reshape(1), row_w.reshape(P, 1)


def kernel(x, norm1_g, w_in, attn_q_gain, attn_k_gain, attn_sink, rel_bias_table, mlstm_conv_w,
           mlstm_conv_b, mlstm_gate_b, mlstm_out_gain, w_branch_attn, w_branch_mlstm, merge_b, w_out,
           norm2_g, w_router_group, b_router_group, w_router_expert, b_router_expert,
           w_expert_gate, w_expert_up, w_expert_down):
    B, S, D = x.shape
    T = B * S
    depth = norm1_g.shape[0]
    x2 = x.reshape(T, D)
    bias = _attn_bias(rel_bias_table)
    for l in range(depth):
        w = w_in[l]
        w_main = jnp.concatenate([w[:, _OFF_MG:_W_IN], w[:, _OFF_QA:_OFF_VM], w[:, _OFF_OM:_OFF_GM]],
                                 axis=1).astype(BF16)
        w_vt = w[:, _OFF_VM:_OFF_OM].T.astype(BF16)
        w_gt = w[:, _OFF_GM:_OFF_MG].T.astype(BF16)
        gate_b = mlstm_gate_b[l].reshape(MLSTM_GATE_W, 1).astype(F32)
        P, VT, GT = _in_proj(x2, norm1_g[l][None, :], w_main, w_vt, w_gt, gate_b)

        qg = jnp.tile(attn_q_gain[l], LANES // HEAD_DIM_ATTN)[None, :]
        kg = jnp.tile(attn_k_gain[l], LANES // HEAD_DIM_ATTN)[None, :]
        ya = _attention(P, bias, attn_sink[l].astype(F32), qg, kg, B, S)

        ym = _mlstm(P, VT, GT, mlstm_conv_w[l], mlstm_conv_b[l][None, :], mlstm_out_gain[l], B, S)

        wr = jnp.zeros((LANES, D), F32)
        wr = wr.at[0:N_GROUPS].set(w_router_group[l].T)
        wr = wr.at[SUBLANES:SUBLANES + N_EXPERTS].set(w_router_expert[l].T)
        wr_hi = wr.astype(BF16)
        wr_lo = (wr - wr_hi.astype(F32)).astype(BF16)
        b_r = jnp.zeros((LANES,), F32)
        b_r = b_r.at[0:N_GROUPS].set(b_router_group[l])
        b_r = b_r.at[SUBLANES:SUBLANES + N_EXPERTS].set(b_router_expert[l])
        x1, h2g, eid, wts = _merge(
            ya, ym, P, merge_b[l][None, :], x2, w_branch_attn[l].astype(BF16),
            w_branch_mlstm[l].astype(BF16), w_out[l].astype(BF16), norm2_g[l][None, :],
            wr_hi, wr_lo, b_r[:, None])

        blk_e, nval, row_tok, row_dst, nused, row_w = _dispatch_plan(eid[0:TOP_K], wts[0:TOP_K], T)
        Y = _moe(blk_e, nval, row_tok, row_dst, nused, h2g, row_w,
                 w_expert_gate[l], w_expert_up[l], w_expert_down[l], T)
        x2 = _combine(x1, Y)
    return x2.reshape(B, S, D)
```

```python
import functools
import math

import jax
import jax.numpy as jnp
from jax import lax
from jax.experimental import pallas as pl
from jax.experimental.pallas import tpu as pltpu

F32 = jnp.float32
BF16 = jnp.bfloat16
I32 = jnp.int32

LANES = 128
SUBLANES = 8
VMEM_LIMIT_BYTES = 56 * 1024 * 1024

D_MODEL = 1024
N_HEADS_ATTN = 8
N_KV_HEADS = 2
HEAD_DIM_ATTN = 64
WINDOW = 128
ATTN_BLOCK = WINDOW
N_REL_BUCKETS = 32
REL_MAX_DISTANCE = 128
N_HEADS_MLSTM = 4
HEAD_DIM_MLSTM = 128
CONV_WIDTH = 5
ATTN_Q_W = N_HEADS_ATTN * HEAD_DIM_ATTN
ATTN_KV_W = N_KV_HEADS * HEAD_DIM_ATTN
MLSTM_W = N_HEADS_MLSTM * HEAD_DIM_MLSTM
MLSTM_GATE_W = 2 * 2 * N_HEADS_MLSTM
N_GROUPS = 4
EXPERTS_PER_GROUP = 8
N_EXPERTS = N_GROUPS * EXPERTS_PER_GROUP
TOP_K = 2
D_EXPERT = 512
RMS_EPS = 1e-6
NEG_INF = -1e30

_OFF_QA = 0
_OFF_VM = ATTN_Q_W + 2 * ATTN_KV_W + 2 * MLSTM_W
_OFF_OM = _OFF_VM + MLSTM_W
_OFF_GM = _OFF_OM + MLSTM_W
_OFF_MG = _OFF_GM + MLSTM_GATE_W
_W_IN = _OFF_MG + 2 * D_MODEL

P_MG = 0
P_QA = 2 * D_MODEL
P_KA = P_QA + ATTN_Q_W
P_VA = P_KA + ATTN_KV_W
P_QM = P_VA + ATTN_KV_W
P_KM = P_QM + MLSTM_W
P_OM = P_KM + MLSTM_W
P_W = P_OM + MLSTM_W

MLSTM_L = 128
TOK_TILE = 512
MOE_BLK = 256
ROW_TILE = SUBLANES

_NT = (((1,), (1,)), ((), ()))


def _dot(a, b):
    return jnp.dot(a, b, preferred_element_type=F32)


def _dot_nt(a, b):
    return lax.dot_general(a, b, _NT, preferred_element_type=F32)


def _inproj_kernel(x_ref, g1_ref, wm_ref, wvt_ref, wgt_ref, gb_ref, p_ref, vt_ref, gt_ref):
    x = x_ref[...]
    h = x * lax.rsqrt(jnp.mean(x * x, axis=-1, keepdims=True) + RMS_EPS) * g1_ref[...]
    hb = h.astype(BF16)
    step = 512
    for c0 in range(0, P_W, step):
        w = min(step, P_W - c0)
        p_ref[:, c0:c0 + w] = _dot(hb, wm_ref[:, c0:c0 + w]).astype(BF16)
    vt = _dot_nt(wvt_ref[...], hb)
    gt = _dot_nt(wgt_ref[...], hb) + gb_ref[...]
    for cc in range(x.shape[0] // LANES):
        vt_ref[cc] = vt[:, cc * LANES:(cc + 1) * LANES].astype(BF16)
        gt_ref[cc] = gt[:, cc * LANES:(cc + 1) * LANES]


def _in_proj(x2, g1, w_main, w_vt, w_gt, gate_b):
    T = x2.shape[0]
    tm = TOK_TILE
    nck = tm // LANES
    return pl.pallas_call(
        _inproj_kernel,
        grid=(T // tm,),
        in_specs=[
            pl.BlockSpec((tm, D_MODEL), lambda i: (i, 0)),
            pl.BlockSpec((1, D_MODEL), lambda i: (0, 0)),
            pl.BlockSpec((D_MODEL, P_W), lambda i: (0, 0)),
            pl.BlockSpec((MLSTM_W, D_MODEL), lambda i: (0, 0)),
            pl.BlockSpec((MLSTM_GATE_W, D_MODEL), lambda i: (0, 0)),
            pl.BlockSpec((MLSTM_GATE_W, 1), lambda i: (0, 0)),
        ],
        out_specs=[
            pl.BlockSpec((tm, P_W), lambda i: (i, 0)),
            pl.BlockSpec((nck, MLSTM_W, LANES), lambda i: (i, 0, 0)),
            pl.BlockSpec((nck, MLSTM_GATE_W, LANES), lambda i: (i, 0, 0)),
        ],
        out_shape=[
            jax.ShapeDtypeStruct((T, P_W), BF16),
            jax.ShapeDtypeStruct((T // LANES, MLSTM_W, LANES), BF16),
            jax.ShapeDtypeStruct((T // LANES, MLSTM_GATE_W, LANES), F32),
        ],
        compiler_params=pltpu.CompilerParams(
            dimension_semantics=("arbitrary",), vmem_limit_bytes=VMEM_LIMIT_BYTES),
        name="in_proj",
    )(x2, g1, w_main, w_vt, w_gt, gate_b)


def _attn_kernel(sink_ref, q_ref, kp_ref, kc_ref, kn_ref, vp_ref, vc_ref, vn_ref,
                 bias_ref, qg_ref, kg_ref, o_ref, *, nb):
    n = pl.program_id(1)
    half = HEAD_DIM_ATTN
    r_i = lax.broadcasted_iota(I32, (LANES, LANES), 0)
    c_i = lax.broadcasted_iota(I32, (LANES, LANES), 1)
    bd = ((r_i // half) == (c_i // half)).astype(BF16)

    def headnorm(t, gain):
        tt = t * t
        hi = tt.astype(BF16)
        lo = (tt - hi.astype(F32)).astype(BF16)
        ms = (_dot(hi, bd) + _dot(lo, bd)) * (1.0 / half)
        return t * lax.rsqrt(ms + RMS_EPS) * gain

    lane = lax.broadcasted_iota(I32, (1, LANES), 1)
    lo_mask = lane < half

    k = jnp.concatenate([kp_ref[...], kc_ref[...], kn_ref[...]], axis=0).astype(F32)
    kn = headnorm(k, kg_ref[...])
    k_nat = kn.astype(BF16)
    k_rot = pltpu.roll(kn, half, axis=1).astype(BF16)
    v = jnp.concatenate([vp_ref[...], vc_ref[...], vn_ref[...]], axis=0).astype(F32)
    v_rot = pltpu.roll(v, half, axis=1)
    zero = jnp.zeros_like(v)
    v_lo = [jnp.where(lo_mask, v, zero).astype(BF16), jnp.where(lo_mask, v_rot, zero).astype(BF16)]
    v_hi = [jnp.where(lo_mask, zero, v_rot).astype(BF16), jnp.where(lo_mask, zero, v).astype(BF16)]

    col = lax.broadcasted_iota(I32, (1, 3 * ATTN_BLOCK), 1)
    col_lo = jnp.where(n > 0, 0, ATTN_BLOCK)
    col_hi = jnp.where(n < nb - 1, 3 * ATTN_BLOCK, 2 * ATTN_BLOCK)
    col_ok = (col >= col_lo) & (col < col_hi)

    for j in range(ATTN_Q_W // LANES):
        qt = q_ref[:, j * LANES:(j + 1) * LANES].astype(F32)
        qn = headnorm(qt, qg_ref[...]) * (HEAD_DIM_ATTN ** -0.5)
        acc = None
        for p in range(2):
            hd = 2 * j + p
            kv = hd // (N_HEADS_ATTN // N_KV_HEADS)
            qm = jnp.where(lo_mask if p == 0 else ~lo_mask, qn, 0.0).astype(BF16)
            kx = k_nat if (kv == p) else k_rot
            s = _dot_nt(qm, kx) + bias_ref[hd]
            s = jnp.where(col_ok, s, NEG_INF)
            sink = sink_ref[hd]
            m = jnp.maximum(jnp.max(s, axis=1, keepdims=True), sink)
            e = jnp.exp(s - m)
            den = jnp.sum(e, axis=1, keepdims=True) + jnp.exp(sink - m)
            probs = (e * (1.0 / den)).astype(BF16)
            vx = v_lo[kv] if p == 0 else v_hi[kv]
            o = _dot(probs, vx)
            acc = o if acc is None else acc + o
        o_ref[:, j * LANES:(j + 1) * LANES] = acc.astype(BF16)


def _attention(P, bias, sink, qg, kg, B, S):
    nb = S // ATTN_BLOCK
    Q = ATTN_BLOCK
    qcol = P_QA // ATTN_Q_W
    kcol = P_KA // ATTN_KV_W
    vcol = P_VA // ATTN_KV_W

    def blk(col, shift):
        def imap(b, n, sink_ref):
            nn = jnp.clip(n + shift, 0, nb - 1)
            return (b * nb + nn, col)
        return pl.BlockSpec((Q, ATTN_KV_W), imap)

    grid_spec = pltpu.PrefetchScalarGridSpec(
        num_scalar_prefetch=1,
        grid=(B, nb),
        in_specs=[
            pl.BlockSpec((Q, ATTN_Q_W), lambda b, n, s: (b * nb + n, qcol)),
            blk(kcol, -1), blk(kcol, 0), blk(kcol, 1),
            blk(vcol, -1), blk(vcol, 0), blk(vcol, 1),
            pl.BlockSpec((N_HEADS_ATTN, Q, 3 * Q), lambda b, n, s: (0, 0, 0)),
            pl.BlockSpec((1, LANES), lambda b, n, s: (0, 0)),
            pl.BlockSpec((1, LANES), lambda b, n, s: (0, 0)),
        ],
        out_specs=pl.BlockSpec((Q, ATTN_Q_W), lambda b, n, s: (b * nb + n, 0)),
    )
    return pl.pallas_call(
        functools.partial(_attn_kernel, nb=nb),
        grid_spec=grid_spec,
        out_shape=jax.ShapeDtypeStruct((B * S, ATTN_Q_W), BF16),
        compiler_params=pltpu.CompilerParams(
            dimension_semantics=("arbitrary", "arbitrary"), vmem_limit_bytes=VMEM_LIMIT_BYTES),
        name="attn",
    )(sink, P, P, P, P, P, P, P, bias, qg, kg)


def _log_sigmoid(x):
    return jnp.minimum(x, 0.0) - jnp.log1p(jnp.exp(-jnp.abs(x)))


def _mlstm_kernel(q_ref, k_ref, vt_ref, o_ref, gt_ref, cwq_ref, cwk_ref, cbq_ref, cbk_ref, og_ref,
                  ym_ref,
                  pad_ref, qs_ref, ks_ref, v1t_ref, ls_ref, cs_ref, ct_ref, m_ref, hs_ref):
    L = MLSTM_L
    S = q_ref.shape[0]
    nc = S // L
    hh = pl.program_id(1)
    H = N_HEADS_MLSTM

    halo = SUBLANES
    zpad = jnp.zeros((halo, LANES), F32)
    pad_ref[0:halo, :] = zpad
    pad_ref[halo + S:2 * halo + S, :] = zpad

    def conv_silu(src_ref, w_ref, b_ref, dst_ref, scale):
        pad_ref[halo:halo + S, :] = src_ref[...].astype(F32)
        for c in range(nc):
            acc = jnp.broadcast_to(b_ref[...], (L, LANES))
            for j in range(CONV_WIDTH):
                off = c * L + halo - CONV_WIDTH // 2 + j
                acc = acc + pad_ref[off:off + L, :] * w_ref[j:j + 1, :]
            y = acc * jax.nn.sigmoid(acc)
            dst_ref[c * L:(c + 1) * L, :] = (y * scale).astype(BF16)

    conv_silu(q_ref, cwq_ref, cbq_ref, qs_ref, 1.0)
    conv_silu(k_ref, cwk_ref, cbk_ref, ks_ref, HEAD_DIM_MLSTM ** -0.5)

    ones_blk = jnp.ones((L, LANES), BF16)
    for c in range(nc):
        v1t_ref[c, 0:L, :] = vt_ref[c]
        v1t_ref[c, L:2 * L, :] = ones_blk

    g_all = gt_ref[...].reshape(nc * MLSTM_GATE_W, LANES)
    ls = _log_sigmoid(g_all)
    lane = lax.broadcasted_iota(I32, (1, LANES), 1)
    pre = ls
    suf = ls
    d = 1
    while d < LANES:
        pre = pre + jnp.where(lane >= d, pltpu.roll(pre, d, axis=1), 0.0)
        suf = suf + jnp.where(lane < LANES - d, pltpu.roll(suf, LANES - d, axis=1), 0.0)
        d *= 2
    ls_ref[...] = ls.reshape(nc, MLSTM_GATE_W, LANES)
    cs_ref[0] = pre.reshape(nc, MLSTM_GATE_W, LANES)
    cs_ref[1] = suf.reshape(nc, MLSTM_GATE_W, LANES)

    ct_ref[...] = jnp.zeros(ct_ref.shape, F32)
    m_ref[...] = jnp.zeros(m_ref.shape, F32)

    r_i = lax.broadcasted_iota(I32, (L, L), 0)
    c_i = lax.broadcasted_iota(I32, (L, L), 1)
    masks = (c_i <= r_i, c_i >= r_i)

    def chunk_step(dr, c, accumulate):
        mask = masks[dr]
        row_i = dr * 2 * H + hh
        row_f = dr * 2 * H + H + hh
        ig_row = gt_ref[c, pl.ds(row_i, 1), :]
        lf_row = ls_ref[c, pl.ds(row_f, 1), :]
        a_row = ig_row - cs_ref[dr, c, pl.ds(row_f, 1), :]
        am = jnp.where(mask, a_row, -jnp.inf)
        cmax = jnp.max(am, axis=1, keepdims=True)
        bcol = jnp.sum(jnp.where(mask, lf_row, 0.0), axis=1, keepdims=True)
        m_row = m_ref[dr, 0:1, :]
        mm = jnp.maximum(cmax, m_row)
        w = jnp.exp(am - mm)
        inter = jnp.exp(m_row - mm)
        clamp = jnp.exp(-(bcol + mm))
        row0 = pl.multiple_of(c * L, L)
        q_c = qs_ref[pl.ds(row0, L), :]
        k_c = ks_ref[pl.ds(row0, L), :]
        sw = (_dot_nt(q_c, k_c) * w).astype(BF16)
        qi = (q_c.astype(F32) * inter).astype(BF16)
        v1t = v1t_ref[c]
        ct = ct_ref[dr]
        lhs = jnp.concatenate([sw, qi], axis=1)
        rhs_t = jnp.concatenate([v1t, ct.astype(BF16)], axis=1)
        nd = _dot_nt(lhs, rhs_t)
        hval = nd[:, 0:L] / jnp.maximum(jnp.abs(nd[:, L:2 * L]), clamp)
        if accumulate:
            hs_ref[pl.ds(row0, L), :] = hs_ref[pl.ds(row0, L), :] + hval
        else:
            hs_ref[pl.ds(row0, L), :] = hval
        m_last = jnp.maximum(jnp.max(a_row, axis=1, keepdims=True), m_row)
        decay = jnp.exp(m_row - m_last)
        ws_row = jnp.exp(a_row - m_last)
        ut = _dot((v1t.astype(F32) * ws_row).astype(BF16), k_c)
        ct_ref[dr] = decay * ct + ut
        m_new = jnp.sum(lf_row, axis=1, keepdims=True) + m_last
        m_ref[dr] = jnp.broadcast_to(m_new, (SUBLANES, LANES))

    def first_half(i, carry):
        chunk_step(0, i, False)
        chunk_step(1, nc - 1 - i, False)
        return carry

    def second_half(i, carry):
        chunk_step(0, i, True)
        chunk_step(1, nc - 1 - i, True)
        return carry

    lax.fori_loop(0, nc // 2, first_half, 0)
    lax.fori_loop(nc // 2, nc, second_half, 0)

    for c in range(nc):
        hsum = hs_ref[c * L:(c + 1) * L, :]
        y = hsum * lax.rsqrt(jnp.mean(hsum * hsum, axis=-1, keepdims=True) + RMS_EPS) * og_ref[0]
        og = jax.nn.sigmoid(o_ref[c * L:(c + 1) * L, :].astype(F32))
        ym_ref[c * L:(c + 1) * L, :] = (y * og).astype(BF16)


def _mlstm(P, VT, GT, conv_w, conv_b, out_gain, B, S):
    H = N_HEADS_MLSTM
    L = MLSTM_L
    nc = S // L
    qcol = P_QM // LANES
    kcol = P_KM // LANES
    ocol = P_OM // LANES
    return pl.pallas_call(
        _mlstm_kernel,
        grid=(B, H),
        in_specs=[
            pl.BlockSpec((S, LANES), lambda b, h: (b, qcol + h)),
            pl.BlockSpec((S, LANES), lambda b, h: (b, kcol + h)),
            pl.BlockSpec((nc, LANES, LANES), lambda b, h: (b, h, 0)),
            pl.BlockSpec((S, LANES), lambda b, h: (b, ocol + h)),
            pl.BlockSpec((nc, MLSTM_GATE_W, LANES), lambda b, h: (b, 0, 0)),
            pl.BlockSpec((CONV_WIDTH, LANES), lambda b, h: (0, h)),
            pl.BlockSpec((CONV_WIDTH, LANES), lambda b, h: (0, H + h)),
            pl.BlockSpec((1, LANES), lambda b, h: (0, h)),
            pl.BlockSpec((1, LANES), lambda b, h: (0, H + h)),
            pl.BlockSpec((1, 1, LANES), lambda b, h: (h, 0, 0)),
        ],
        out_specs=pl.BlockSpec((S, LANES), lambda b, h: (b, h)),
        out_shape=jax.ShapeDtypeStruct((B * S, MLSTM_W), BF16),
        scratch_shapes=[
            pltpu.VMEM((S + 2 * SUBLANES, LANES), F32),
            pltpu.VMEM((S, LANES), BF16),
            pltpu.VMEM((S, LANES), BF16),
            pltpu.VMEM((nc, 2 * L, LANES), BF16),
            pltpu.VMEM((nc, MLSTM_GATE_W, LANES), F32),
            pltpu.VMEM((2, nc, MLSTM_GATE_W, LANES), F32),
            pltpu.VMEM((2, 2 * L, LANES), F32),
            pltpu.VMEM((2, SUBLANES, LANES), F32),
            pltpu.VMEM((S, LANES), F32),
        ],
        compiler_params=pltpu.CompilerParams(
            dimension_semantics=("arbitrary", "arbitrary"), vmem_limit_bytes=VMEM_LIMIT_BYTES),
        name="mlstm",
    )(P, P, VT, P, GT, conv_w, conv_w, conv_b, conv_b, out_gain[:, None, :])


def _merge_kernel(ya_ref, ym_ref, mga_ref, mgm_ref, mb_ref, x_ref, wa_ref, wm_ref, wo_ref, g2_ref,
                  wrh_ref, wrl_ref, br_ref, tri_ref, x1_ref, h2_ref, eid_ref, wt_ref, cnt_ref, run_ref):
    tm = x_ref.shape[0]

    @pl.when(pl.program_id(0) == 0)
    def _():
        run_ref[...] = jnp.zeros(run_ref.shape, F32)
    a = _dot(ya_ref[...], wa_ref[...])
    mm = _dot(ym_ref[...], wm_ref[...])
    ga = jax.nn.sigmoid(mga_ref[...].astype(F32) + mb_ref[:, 0:D_MODEL])
    gm = jax.nn.sigmoid(mgm_ref[...].astype(F32) + mb_ref[:, D_MODEL:2 * D_MODEL])
    u = (ga * a + gm * mm).astype(BF16)
    x1 = x_ref[...] + _dot(u, wo_ref[...])
    x1_ref[...] = x1
    h2 = x1 * lax.rsqrt(jnp.mean(x1 * x1, axis=-1, keepdims=True) + RMS_EPS) * g2_ref[...]
    for j in range(D_MODEL // LANES):
        h2_ref[pl.ds(j, tm, stride=ROW_TILE), :] = h2[:, j * LANES:(j + 1) * LANES]

    hi = h2.astype(BF16)
    lo = (h2 - hi.astype(F32)).astype(BF16)
    lg = (_dot_nt(wrh_ref[...], hi) + _dot_nt(wrh_ref[...], lo) + _dot_nt(wrl_ref[...], hi)
          + br_ref[...])
    row = lax.broadcasted_iota(I32, (SUBLANES, tm), 0)
    gl = jnp.where(row < N_GROUPS, lg[0:SUBLANES], -jnp.inf)
    gmax = jnp.max(gl, axis=0, keepdims=True)
    g_idx = jnp.min(jnp.where(gl == gmax, row, SUBLANES), axis=0, keepdims=True)
    p_group = 1.0 / jnp.sum(jnp.exp(gl - gmax), axis=0, keepdims=True)
    e_in = lg[SUBLANES * N_GROUPS:SUBLANES * (N_GROUPS + 1)]
    for g in range(N_GROUPS - 2, -1, -1):
        e_in = jnp.where(g_idx == g, lg[SUBLANES * (g + 1):SUBLANES * (g + 2)], e_in)
    v1 = jnp.max(e_in, axis=0, keepdims=True)
    i1 = jnp.min(jnp.where(e_in == v1, row, SUBLANES), axis=0, keepdims=True)
    e2 = jnp.where(row == i1, -jnp.inf, e_in)
    v2 = jnp.max(e2, axis=0, keepdims=True)
    i2 = jnp.min(jnp.where(e2 == v2, row, SUBLANES), axis=0, keepdims=True)
    t = jnp.exp(v2 - v1)
    w1 = p_group / (1.0 + t)
    w2 = p_group * t / (1.0 + t)
    id1 = g_idx * EXPERTS_PER_GROUP + i1
    id2 = g_idx * EXPERTS_PER_GROUP + i2
    wt_ref[...] = jnp.where(row == 0, w1, jnp.where(row == 1, w2, 0.0))

    erow = lax.broadcasted_iota(I32, (N_EXPERTS, tm), 0)
    oh1 = (erow == id1).astype(F32)
    oh2 = (erow == id2).astype(F32)
    c1 = _dot(oh1.astype(BF16), tri_ref[...])
    c2 = _dot(oh2.astype(BF16), tri_ref[...])
    tot1 = jnp.sum(oh1, axis=1, keepdims=True)
    tot2 = jnp.sum(oh2, axis=1, keepdims=True)
    base = run_ref[:, 0:1]
    r1 = jnp.sum(oh1 * (base + c1), axis=0, keepdims=True)
    r2 = jnp.sum(oh2 * (base + tot1 + c2), axis=0, keepdims=True)
    run_ref[...] = run_ref[...] + (tot1 + tot2)
    cnt_ref[...] = run_ref[...]
    eid_ref[...] = jnp.where(row == 0, id1, jnp.where(row == 1, id2, jnp.where(
        row == 2, r1.astype(I32), jnp.where(row == 3, r2.astype(I32), 0))))


def _merge(ya, ym, P, merge_b, x2, w_a, w_m, w_o, g2, wr_hi, wr_lo, b_r):
    T = x2.shape[0]
    tm = TOK_TILE
    const = lambda i: (0, 0)
    ti = jnp.arange(tm, dtype=I32)
    tri = (ti[:, None] < ti[None, :]).astype(BF16)
    return pl.pallas_call(
        _merge_kernel,
        grid=(T // tm,),
        in_specs=[
            pl.BlockSpec((tm, ATTN_Q_W), lambda i: (i, 0)),
            pl.BlockSpec((tm, MLSTM_W), lambda i: (i, 0)),
            pl.BlockSpec((tm, D_MODEL), lambda i: (i, 0)),
            pl.BlockSpec((tm, D_MODEL), lambda i: (i, 1)),
            pl.BlockSpec((1, 2 * D_MODEL), const),
            pl.BlockSpec((tm, D_MODEL), lambda i: (i, 0)),
            pl.BlockSpec((ATTN_Q_W, D_MODEL), const),
            pl.BlockSpec((MLSTM_W, D_MODEL), const),
            pl.BlockSpec((D_MODEL, D_MODEL), const),
            pl.BlockSpec((1, D_MODEL), const),
            pl.BlockSpec((LANES, D_MODEL), const),
            pl.BlockSpec((LANES, D_MODEL), const),
            pl.BlockSpec((LANES, 1), const),
            pl.BlockSpec((tm, tm), const),
        ],
        out_specs=[
            pl.BlockSpec((tm, D_MODEL), lambda i: (i, 0)),
            pl.BlockSpec((tm * ROW_TILE, LANES), lambda i: (i, 0)),
            pl.BlockSpec((SUBLANES, tm), lambda i: (0, i)),
            pl.BlockSpec((SUBLANES, tm), lambda i: (0, i)),
            pl.BlockSpec((N_EXPERTS, LANES), const),
        ],
        out_shape=[
            jax.ShapeDtypeStruct((T, D_MODEL), F32),
            jax.ShapeDtypeStruct((T * ROW_TILE, LANES), F32),
            jax.ShapeDtypeStruct((SUBLANES, T), I32),
            jax.ShapeDtypeStruct((SUBLANES, T), F32),
            jax.ShapeDtypeStruct((N_EXPERTS, LANES), F32),
        ],
        scratch_shapes=[pltpu.VMEM((N_EXPERTS, LANES), F32)],
        compiler_params=pltpu.CompilerParams(
            dimension_semantics=("arbitrary",), vmem_limit_bytes=VMEM_LIMIT_BYTES),
        name="merge",
    )(ya, ym, P, P, merge_b, x2, w_a, w_m, w_o, g2, wr_hi, wr_lo, b_r, tri)


def _inv_kernel(dest_ref, pad_hbm, inv_hbm, inv_smem, sem, *, chunk):
    c = pl.program_id(0)

    @pl.when(c == 0)
    def _():
        cp = pltpu.make_async_copy(pad_hbm, inv_smem, sem)
        cp.start()
        cp.wait()

    def body(i, carry):
        inv_smem[dest_ref[0, 0, i]] = c * chunk + i
        return carry
    lax.fori_loop(0, chunk, body, 0, unroll=16)

    @pl.when(c == pl.num_programs(0) - 1)
    def _():
        cp = pltpu.make_async_copy(inv_smem, inv_hbm, sem)
        cp.start()
        cp.wait()


def _inverse_map(dest_flat, n_rows, pad_base):
    A = dest_flat.shape[0]
    chunk = min(A, 4096)
    nch = A // chunk
    pad_ids = pad_base + (jnp.arange(n_rows, dtype=I32) & (2 * MOE_BLK - 1))
    return pl.pallas_call(
        functools.partial(_inv_kernel, chunk=chunk),
        grid=(nch,),
        in_specs=[
            pl.BlockSpec((1, 1, chunk), lambda c: (c, 0, 0), memory_space=pltpu.SMEM),
            pl.BlockSpec(memory_space=pl.ANY),
        ],
        out_specs=pl.BlockSpec(memory_space=pl.ANY),
        scratch_shapes=[pltpu.SMEM((n_rows,), I32), pltpu.SemaphoreType.DMA],
        out_shape=jax.ShapeDtypeStruct((n_rows,), I32),
        compiler_params=pltpu.CompilerParams(dimension_semantics=("arbitrary",)),
        name="inverse_map",
    )(dest_flat.reshape(nch, 1, chunk), pad_ids)


def _moe_kernel(blk_e_ref, nused_ref,
                invp_ref, invc_ref, invn_ref, h2_hbm, wg_ref, wu_ref, wd_ref,
                y_hbm,
                xg_ref, ys_ref, xs_ref, wgb_ref, wub_ref, wdb_ref, gsem, ssem, *, n_tok):
    blk = MOE_BLK
    b = pl.program_id(0)
    nused = nused_ref[0]
    slot = lax.rem(b, 2)
    oslot = 1 - slot
    tok_mask = n_tok - 1
    n_lane_tiles = D_MODEL // LANES

    def gather_copy(r, a, slot_):
        return pltpu.make_async_copy(
            h2_hbm.at[pl.ds((a & tok_mask) * ROW_TILE, ROW_TILE), :],
            xg_ref.at[slot_, pl.ds(r * ROW_TILE, ROW_TILE), :],
            gsem.at[slot_])

    def scatter_copy(r, a, slot_):
        return pltpu.make_async_copy(
            ys_ref.at[slot_, pl.ds(r * ROW_TILE, ROW_TILE), :],
            y_hbm.at[pl.ds(a * ROW_TILE, ROW_TILE), :],
            ssem.at[slot_])

    def wait_rows(copy_fn, slot_):
        for _ in range(blk):
            copy_fn(0, 0, slot_).wait()

    @pl.when(b < nused)
    def _():
        @pl.when(b == 0)
        def _():
            ys_ref[...] = jnp.zeros(ys_ref.shape, F32)

            def prologue(r, carry):
                scatter_copy(r, TOP_K * n_tok + r, 0).start()
                gather_copy(r, invc_ref[0, 0, r], 0).start()
                return carry
            lax.fori_loop(0, blk, prologue, 0)

        changed = (b == 0) | (blk_e_ref[b] != blk_e_ref[jnp.maximum(b - 1, 0)])

        @pl.when(changed)
        def _():
            wgb_ref[...] = wg_ref[0].astype(BF16)
            wub_ref[...] = wu_ref[0].astype(BF16)
            wdb_ref[...] = wd_ref[0].astype(BF16)

        wait_rows(gather_copy, slot)
        per = blk // n_lane_tiles
        for j in range(n_lane_tiles):
            xs_ref[:, j * LANES:(j + 1) * LANES] = (
                xg_ref[slot, pl.ds(j, blk, stride=ROW_TILE), :].astype(BF16))
            for r in range(j * per, (j + 1) * per):
                gather_copy(r, invn_ref[0, 0, r], oslot).start()
        x = xs_ref[...]
        nw = 256
        acts = []
        r0 = 0
        n_up = D_EXPERT // nw
        n_dn = D_MODEL // nw
        per_up = blk // (2 * n_up)
        per_dn = (blk - n_up * per_up) // n_dn
        for n in range(n_up):
            g = _dot(x, wgb_ref[:, n * nw:(n + 1) * nw])
            u = _dot(x, wub_ref[:, n * nw:(n + 1) * nw])
            acts.append((g * jax.nn.sigmoid(g) * u).astype(BF16))
            for r in range(r0, r0 + per_up):
                scatter_copy(r, invp_ref[0, 0, r], oslot).start()
            r0 += per_up
        act = jnp.concatenate(acts, axis=1)
        wait_rows(scatter_copy, slot)
        for n in range(n_dn):
            y = _dot(act, wdb_ref[:, n * nw:(n + 1) * nw])
            for jj in range(nw // LANES):
                j = n * (nw // LANES) + jj
                ys_ref[slot, pl.ds(j, blk, stride=ROW_TILE), :] = y[:, jj * LANES:(jj + 1) * LANES]
            for r in range(r0, r0 + per_dn):
                scatter_copy(r, invp_ref[0, 0, r], oslot).start()
            r0 += per_dn
        assert r0 == blk

        @pl.when(b == nused - 1)
        def _():
            def epilogue(r, carry):
                scatter_copy(r, invc_ref[0, 0, r], slot).start()
                return carry
            lax.fori_loop(0, blk, epilogue, 0)
            wait_rows(scatter_copy, oslot)
            wait_rows(scatter_copy, slot)
            wait_rows(gather_copy, oslot)


def _moe(blk_e, nused, inv, h2g, w_gate, w_up, w_down, T):
    blk = MOE_BLK
    assert T & (T - 1) == 0, "token count must be a power of two for the id -> token mask"
    nblk = inv.shape[0] // blk
    inv3 = inv.reshape(nblk, 1, blk)
    virt = (TOP_K * T + blk + jnp.arange(blk, dtype=I32)).reshape(1, 1, blk)
    invp3 = jnp.concatenate([virt, inv3[:-1]], axis=0)
    grid_spec = pltpu.PrefetchScalarGridSpec(
        num_scalar_prefetch=2,
        grid=(nblk,),
        in_specs=[
            pl.BlockSpec((1, 1, blk), lambda b, e, n: (b, 0, 0), memory_space=pltpu.SMEM),
            pl.BlockSpec((1, 1, blk), lambda b, e, n: (b, 0, 0), memory_space=pltpu.SMEM),
            pl.BlockSpec((1, 1, blk), lambda b, e, n: (jnp.minimum(b + 1, nblk - 1), 0, 0),
                         memory_space=pltpu.SMEM),
            pl.BlockSpec(memory_space=pl.ANY),
            pl.BlockSpec((1, D_MODEL, D_EXPERT), lambda b, e, n: (e[b], 0, 0)),
            pl.BlockSpec((1, D_MODEL, D_EXPERT), lambda b, e, n: (e[b], 0, 0)),
            pl.BlockSpec((1, D_EXPERT, D_MODEL), lambda b, e, n: (e[b], 0, 0)),
        ],
        out_specs=pl.BlockSpec(memory_space=pl.ANY),
        scratch_shapes=[
            pltpu.VMEM((2, blk * ROW_TILE, LANES), F32),
            pltpu.VMEM((2, blk * ROW_TILE, LANES), F32),
            pltpu.VMEM((blk, D_MODEL), BF16),
            pltpu.VMEM((D_MODEL, D_EXPERT), BF16),
            pltpu.VMEM((D_MODEL, D_EXPERT), BF16),
            pltpu.VMEM((D_EXPERT, D_MODEL), BF16),
            pltpu.SemaphoreType.DMA((2,)),
            pltpu.SemaphoreType.DMA((2,)),
        ],
    )
    return pl.pallas_call(
        functools.partial(_moe_kernel, n_tok=T),
        grid_spec=grid_spec,
        out_shape=jax.ShapeDtypeStruct(((TOP_K * T + 2 * blk) * ROW_TILE, LANES), F32),
        compiler_params=pltpu.CompilerParams(
            dimension_semantics=("arbitrary",), vmem_limit_bytes=VMEM_LIMIT_BYTES),
        name="moe",
    )(blk_e, nused, invp3, inv3, inv3, h2g, w_gate, w_up, w_down)


def _combine_kernel(x1_ref, y0_ref, y1_ref, wt_ref, o_ref):
    tm = x1_ref.shape[0]
    r_i = lax.broadcasted_iota(I32, (LANES, LANES), 0)
    c_i = lax.broadcasted_iota(I32, (LANES, LANES), 1)
    eye = r_i == c_i
    for c in range(tm // LANES):
        rows = pl.ds(c * LANES, LANES)
        w0 = jnp.sum(jnp.where(eye, wt_ref[0:1, c * LANES:(c + 1) * LANES], 0.0), axis=1, keepdims=True)
        w1 = jnp.sum(jnp.where(eye, wt_ref[1:2, c * LANES:(c + 1) * LANES], 0.0), axis=1, keepdims=True)
        for j in range(D_MODEL // LANES):
            sl = slice(j * LANES, (j + 1) * LANES)
            tile = pl.ds(c * LANES * ROW_TILE + j, LANES, stride=ROW_TILE)
            o_ref[rows, sl] = x1_ref[rows, sl] + w0 * y0_ref[tile, :] + w1 * y1_ref[tile, :]


def _combine(x1, Y, wts):
    T = x1.shape[0]
    tm = TOK_TILE
    nt = T // tm
    return pl.pallas_call(
        _combine_kernel,
        grid=(nt,),
        in_specs=[
            pl.BlockSpec((tm, D_MODEL), lambda i: (i, 0)),
            pl.BlockSpec((tm * ROW_TILE, LANES), lambda i: (i, 0)),
            pl.BlockSpec((tm * ROW_TILE, LANES), lambda i: (i + nt, 0)),
            pl.BlockSpec((SUBLANES, tm), lambda i: (0, i)),
        ],
        out_specs=pl.BlockSpec((tm, D_MODEL), lambda i: (i, 0)),
        out_shape=jax.ShapeDtypeStruct((T, D_MODEL), F32),
        compiler_params=pltpu.CompilerParams(
            dimension_semantics=("arbitrary",), vmem_limit_bytes=VMEM_LIMIT_BYTES),
        name="combine",
    )(x1, Y, Y, wts)


def _t5_bucket(rel):
    half = N_REL_BUCKETS // 2
    max_exact = half // 2
    sign = jnp.where(rel > 0, half, 0)
    n = jnp.abs(rel)
    nf = jnp.maximum(n, 1).astype(jnp.float32)
    large = max_exact + (jnp.log(nf / max_exact) / math.log(REL_MAX_DISTANCE / max_exact)
                         * (half - max_exact)).astype(jnp.int32)
    large = jnp.minimum(large, half - 1)
    return sign + jnp.where(n < max_exact, n, large)


def _attn_bias(rel_table):
    qi = jnp.arange(ATTN_BLOCK)
    ks = jnp.arange(3 * ATTN_BLOCK) - ATTN_BLOCK
    rel = ks[None, :] - qi[:, None]
    bias = jnp.transpose(rel_table[_t5_bucket(rel)].astype(F32), (2, 0, 1))
    return jnp.where((jnp.abs(rel) <= WINDOW)[None], bias, NEG_INF)


def _block_table(counts, T):
    blk = MOE_BLK
    nblk = (TOP_K * T) // blk + N_EXPERTS
    pcounts = ((counts + blk - 1) // blk) * blk
    pends = jnp.cumsum(pcounts)
    pstarts = (pends - pcounts).astype(I32)
    nused = (pends[-1] // blk).astype(I32)
    first_row = jnp.arange(nblk, dtype=I32) * blk
    blk_e = jnp.sum((first_row[:, None] >= pends[None, :]).astype(I32), axis=1)
    blk_e = jnp.minimum(blk_e, N_EXPERTS - 1)
    last_e = jnp.sum((((nused - 1) * blk) >= pends).astype(I32))
    blk_e = jnp.where(jnp.arange(nblk) < nused, blk_e, jnp.minimum(last_e, N_EXPERTS - 1)).astype(I32)
    return pstarts, blk_e, nused.reshape(1), nblk * blk


def kernel(x, norm1_g, w_in, attn_q_gain, attn_k_gain, attn_sink, rel_bias_table, mlstm_conv_w,
           mlstm_conv_b, mlstm_gate_b, mlstm_out_gain, w_branch_attn, w_branch_mlstm, merge_b, w_out,
           norm2_g, w_router_group, b_router_group, w_router_expert, b_router_expert,
           w_expert_gate, w_expert_up, w_expert_down):
    B, S, D = x.shape
    T = B * S
    depth = norm1_g.shape[0]
    x2 = x.reshape(T, D)
    bias = _attn_bias(rel_bias_table)
    for l in range(depth):
        w = w_in[l]
        w_main = jnp.concatenate([w[:, _OFF_MG:_W_IN], w[:, _OFF_QA:_OFF_VM], w[:, _OFF_OM:_OFF_GM]],
                                 axis=1).astype(BF16)
        w_vt = w[:, _OFF_VM:_OFF_OM].T.astype(BF16)
        w_gt = w[:, _OFF_GM:_OFF_MG].T.astype(BF16)
        gate_b = mlstm_gate_b[l].reshape(MLSTM_GATE_W, 1).astype(F32)
        P, VT, GT = _in_proj(x2, norm1_g[l][None, :], w_main, w_vt, w_gt, gate_b)

        qg = jnp.tile(attn_q_gain[l], LANES // HEAD_DIM_ATTN)[None, :]
        kg = jnp.tile(attn_k_gain[l], LANES // HEAD_DIM_ATTN)[None, :]
        ya = _attention(P, bias, attn_sink[l].astype(F32), qg, kg, B, S)

        ym = _mlstm(P, VT, GT, mlstm_conv_w[l], mlstm_conv_b[l][None, :], mlstm_out_gain[l], B, S)

        wr = jnp.zeros((LANES, D), F32)
        wr = wr.at[0:N_GROUPS].set(w_router_group[l].T)
        wr = wr.at[SUBLANES:SUBLANES + N_EXPERTS].set(w_router_expert[l].T)
        wr_hi = wr.astype(BF16)
        wr_lo = (wr - wr_hi.astype(F32)).astype(BF16)
        b_r = jnp.zeros((LANES,), F32)
        b_r = b_r.at[0:N_GROUPS].set(b_router_group[l])
        b_r = b_r.at[SUBLANES:SUBLANES + N_EXPERTS].set(b_router_expert[l])
        x1, h2g, route, wts, cnt = _merge(
            ya, ym, P, merge_b[l][None, :], x2, w_branch_attn[l].astype(BF16),
            w_branch_mlstm[l].astype(BF16), w_out[l].astype(BF16), norm2_g[l][None, :],
            wr_hi, wr_lo, b_r[:, None])

        pstarts, blk_e, nused, n_rows = _block_table(cnt[:, 0].astype(I32), T)
        eid_flat = route[0:TOP_K].reshape(TOP_K * T)
        seg_start = jnp.sum(jnp.where(eid_flat[None, :] == jnp.arange(N_EXPERTS, dtype=I32)[:, None],
                                      pstarts[:, None], 0), axis=0)
        dest = seg_start + route[TOP_K:2 * TOP_K].reshape(TOP_K * T)
        inv = _inverse_map(dest, n_rows, TOP_K * T)
        Y = _moe(blk_e, nused, inv, h2g, w_expert_gate[l], w_expert_up[l], w_expert_down[l], T)
        x2 = _combine(x1, Y, wts)
    return x2.reshape(B, S, D)
```

```python
import functools
import math

import jax
import jax.numpy as jnp
from jax import lax
from jax.experimental import pallas as pl
from jax.experimental.pallas import tpu as pltpu

F32 = jnp.float32
BF16 = jnp.bfloat16
I32 = jnp.int32

LANES = 128
SUBLANES = 8
VMEM_LIMIT_BYTES = 56 * 1024 * 1024

D_MODEL = 1024
N_HEADS_ATTN = 8
N_KV_HEADS = 2
HEAD_DIM_ATTN = 64
WINDOW = 128
ATTN_BLOCK = WINDOW
N_REL_BUCKETS = 32
REL_MAX_DISTANCE = 128
N_HEADS_MLSTM = 4
HEAD_DIM_MLSTM = 128
CONV_WIDTH = 5
ATTN_Q_W = N_HEADS_ATTN * HEAD_DIM_ATTN
ATTN_KV_W = N_KV_HEADS * HEAD_DIM_ATTN
MLSTM_W = N_HEADS_MLSTM * HEAD_DIM_MLSTM
MLSTM_GATE_W = 2 * 2 * N_HEADS_MLSTM
N_GROUPS = 4
EXPERTS_PER_GROUP = 8
N_EXPERTS = N_GROUPS * EXPERTS_PER_GROUP
TOP_K = 2
D_EXPERT = 512
RMS_EPS = 1e-6
NEG_INF = -1e30

_OFF_QA = 0
_OFF_VM = ATTN_Q_W + 2 * ATTN_KV_W + 2 * MLSTM_W
_OFF_OM = _OFF_VM + MLSTM_W
_OFF_GM = _OFF_OM + MLSTM_W
_OFF_MG = _OFF_GM + MLSTM_GATE_W
_W_IN = _OFF_MG + 2 * D_MODEL

P_MG = 0
P_QA = 2 * D_MODEL
P_KA = P_QA + ATTN_Q_W
P_VA = P_KA + ATTN_KV_W
P_QM = P_VA + ATTN_KV_W
P_KM = P_QM + MLSTM_W
P_OM = P_KM + MLSTM_W
P_W = P_OM + MLSTM_W

ATTN_ROWS = 32
MLSTM_L = 128
TOK_TILE = 512
MOE_BLK = 256
ROW_TILE = SUBLANES

_NT = (((1,), (1,)), ((), ()))


def _dot(a, b):
    return jnp.dot(a, b, preferred_element_type=F32)


def _dot_nt(a, b):
    return lax.dot_general(a, b, _NT, preferred_element_type=F32)


def _inproj_kernel(x_ref, g1_ref, wm_ref, wvt_ref, wgt_ref, gb_ref, p_ref, vt_ref, gt_ref):
    x = x_ref[...]
    h = x * lax.rsqrt(jnp.mean(x * x, axis=-1, keepdims=True) + RMS_EPS) * g1_ref[...]
    hb = h.astype(BF16)
    step = 512
    for c0 in range(0, P_W, step):
        w = min(step, P_W - c0)
        p_ref[:, c0:c0 + w] = _dot(hb, wm_ref[:, c0:c0 + w]).astype(BF16)
    vt = _dot_nt(wvt_ref[...], hb)
    gt = _dot_nt(wgt_ref[...], hb) + gb_ref[...]
    for cc in range(x.shape[0] // LANES):
        vt_ref[cc] = vt[:, cc * LANES:(cc + 1) * LANES].astype(BF16)
        gt_ref[cc] = gt[:, cc * LANES:(cc + 1) * LANES]


def _in_proj(x2, g1, w_main, w_vt, w_gt, gate_b):
    T = x2.shape[0]
    tm = TOK_TILE
    nck = tm // LANES
    return pl.pallas_call(
        _inproj_kernel,
        grid=(T // tm,),
        in_specs=[
            pl.BlockSpec((tm, D_MODEL), lambda i: (i, 0)),
            pl.BlockSpec((1, D_MODEL), lambda i: (0, 0)),
            pl.BlockSpec((D_MODEL, P_W), lambda i: (0, 0)),
            pl.BlockSpec((MLSTM_W, D_MODEL), lambda i: (0, 0)),
            pl.BlockSpec((MLSTM_GATE_W, D_MODEL), lambda i: (0, 0)),
            pl.BlockSpec((MLSTM_GATE_W, 1), lambda i: (0, 0)),
        ],
        out_specs=[
            pl.BlockSpec((tm, P_W), lambda i: (i, 0)),
            pl.BlockSpec((nck, MLSTM_W, LANES), lambda i: (i, 0, 0)),
            pl.BlockSpec((nck, MLSTM_GATE_W, LANES), lambda i: (i, 0, 0)),
        ],
        out_shape=[
            jax.ShapeDtypeStruct((T, P_W), BF16),
            jax.ShapeDtypeStruct((T // LANES, MLSTM_W, LANES), BF16),
            jax.ShapeDtypeStruct((T // LANES, MLSTM_GATE_W, LANES), F32),
        ],
        compiler_params=pltpu.CompilerParams(
            dimension_semantics=("arbitrary",), vmem_limit_bytes=VMEM_LIMIT_BYTES),
        name="in_proj",
    )(x2, g1, w_main, w_vt, w_gt, gate_b)


def _attn_kernel(sink_ref, q_ref, kp_ref, kc_ref, kn_ref, vp_ref, vc_ref, vn_ref,
                 bias_ref, qg_ref, kg_ref, o_ref):
    half = HEAD_DIM_ATTN
    Q = ATTN_BLOCK
    n_tiles = ATTN_Q_W // LANES
    group = N_HEADS_ATTN // N_KV_HEADS
    r_i = lax.broadcasted_iota(I32, (LANES, LANES), 0)
    c_i = lax.broadcasted_iota(I32, (LANES, LANES), 1)
    bd = ((r_i // half) == (c_i // half)).astype(BF16)
    lane = lax.broadcasted_iota(I32, (1, LANES), 1)
    lo_mask = lane < half

    parts = [kp_ref[...], kc_ref[...], kn_ref[...]] + [q_ref[:, j * LANES:(j + 1) * LANES] for j in range(n_tiles)]
    t = jnp.concatenate(parts, axis=0).astype(F32)
    tt = t * t
    hi = tt.astype(BF16)
    lo = (tt - hi.astype(F32)).astype(BF16)
    nrow = t.shape[0]
    ms2 = _dot(jnp.concatenate([hi, lo], axis=0), bd)
    ms = (ms2[0:nrow] + ms2[nrow:2 * nrow]) * (1.0 / half)
    tn = t * lax.rsqrt(ms + RMS_EPS)
    kn = tn[0:3 * Q] * kg_ref[...]
    k_nat = kn.astype(BF16)
    k_rot = pltpu.roll(kn, half, axis=1).astype(BF16)
    qn = [tn[3 * Q + j * Q:3 * Q + (j + 1) * Q] * qg_ref[...] * (HEAD_DIM_ATTN ** -0.5) for j in range(n_tiles)]

    v = jnp.concatenate([vp_ref[...], vc_ref[...], vn_ref[...]], axis=0).astype(F32)
    v_rot = pltpu.roll(v, half, axis=1)
    zero = jnp.zeros_like(v)
    v_pair = [
        jnp.concatenate([jnp.where(lo_mask, v, zero), jnp.where(lo_mask, zero, v_rot)], axis=0).astype(BF16),
        jnp.concatenate([jnp.where(lo_mask, v_rot, zero), jnp.where(lo_mask, zero, v)], axis=0).astype(BF16),
    ]

    def q_masked(hd):
        keep = lo_mask if hd % 2 == 0 else ~lo_mask
        return jnp.where(keep, qn[hd // 2], 0.0).astype(BF16)

    heads_nat = [hd for hd in range(N_HEADS_ATTN) if (hd // group) == (hd % 2)]
    heads_rot = [hd for hd in range(N_HEADS_ATTN) if (hd // group) != (hd % 2)]
    scores = {}
    for kx, hds in ((k_nat, heads_nat), (k_rot, heads_rot)):
        s_all = _dot_nt(jnp.concatenate([q_masked(hd) for hd in hds], axis=0), kx)
        for i, hd in enumerate(hds):
            scores[hd] = s_all[i * Q:(i + 1) * Q]

    for j in range(n_tiles):
        probs = []
        for hd in (2 * j, 2 * j + 1):
            s = scores[hd] + bias_ref[0, hd]
            sink = sink_ref[hd]
            m = jnp.maximum(jnp.max(s, axis=1, keepdims=True), sink)
            e = jnp.exp(s - m)
            den = jnp.sum(e, axis=1, keepdims=True) + jnp.exp(sink - m)
            probs.append((e * (1.0 / den)).astype(BF16))
        kv = (2 * j) // group
        o_ref[:, j * LANES:(j + 1) * LANES] = _dot(jnp.concatenate(probs, axis=1), v_pair[kv]).astype(BF16)


def _attention(P, bias, sink, qg, kg, B, S):
    nb = S // ATTN_BLOCK
    Q = ATTN_BLOCK
    qcol = P_QA // ATTN_Q_W
    kcol = P_KA // ATTN_KV_W
    vcol = P_VA // ATTN_KV_W

    def blk(col, shift):
        def imap(b, n, sink_ref):
            nn = jnp.clip(n + shift, 0, nb - 1)
            return (b * nb + nn, col)
        return pl.BlockSpec((Q, ATTN_KV_W), imap)

    grid_spec = pltpu.PrefetchScalarGridSpec(
        num_scalar_prefetch=1,
        grid=(B, nb),
        in_specs=[
            pl.BlockSpec((Q, ATTN_Q_W), lambda b, n, s: (b * nb + n, qcol)),
            blk(kcol, -1), blk(kcol, 0), blk(kcol, 1),
            blk(vcol, -1), blk(vcol, 0), blk(vcol, 1),
            pl.BlockSpec((1, N_HEADS_ATTN, Q, 3 * Q),
                         lambda b, n, s: ((n == 0).astype(I32) + 2 * (n == nb - 1).astype(I32), 0, 0, 0)),
            pl.BlockSpec((1, LANES), lambda b, n, s: (0, 0)),
            pl.BlockSpec((1, LANES), lambda b, n, s: (0, 0)),
        ],
        out_specs=pl.BlockSpec((Q, ATTN_Q_W), lambda b, n, s: (b * nb + n, 0)),
    )
    return pl.pallas_call(
        _attn_kernel,
        grid_spec=grid_spec,
        out_shape=jax.ShapeDtypeStruct((B * S, ATTN_Q_W), BF16),
        compiler_params=pltpu.CompilerParams(
            dimension_semantics=("arbitrary", "arbitrary"), vmem_limit_bytes=VMEM_LIMIT_BYTES),
        name="attn",
    )(sink, P, P, P, P, P, P, P, bias, qg, kg)


def _log_sigmoid(x):
    return jnp.minimum(x, 0.0) - jnp.log1p(jnp.exp(-jnp.abs(x)))


def _mlstm_kernel(q_ref, k_ref, vt_ref, o_ref, gt_ref, cwq_ref, cwk_ref, cbq_ref, cbk_ref, og_ref,
                  ym_ref,
                  pad_ref, qs_ref, ks_ref, v1t_ref, ls_ref, cs_ref, ct_ref, m_ref, hs_ref):
    L = MLSTM_L
    S = q_ref.shape[0]
    nc = S // L
    hh = pl.program_id(1)
    H = N_HEADS_MLSTM

    halo = SUBLANES
    zpad = jnp.zeros((halo, LANES), F32)
    pad_ref[0:halo, :] = zpad
    pad_ref[halo + S:2 * halo + S, :] = zpad

    def conv_silu(src_ref, w_ref, b_ref, dst_ref, scale):
        pad_ref[halo:halo + S, :] = src_ref[...].astype(F32)
        for c in range(nc):
            acc = jnp.broadcast_to(b_ref[...], (L, LANES))
            for j in range(CONV_WIDTH):
                off = c * L + halo - CONV_WIDTH // 2 + j
                acc = acc + pad_ref[off:off + L, :] * w_ref[j:j + 1, :]
            y = acc * jax.nn.sigmoid(acc)
            dst_ref[c * L:(c + 1) * L, :] = (y * scale).astype(BF16)

    conv_silu(q_ref, cwq_ref, cbq_ref, qs_ref, 1.0)
    conv_silu(k_ref, cwk_ref, cbk_ref, ks_ref, HEAD_DIM_MLSTM ** -0.5)

    ones_blk = jnp.ones((L, LANES), BF16)
    for c in range(nc):
        v1t_ref[c, 0:L, :] = vt_ref[c]
        v1t_ref[c, L:2 * L, :] = ones_blk

    g_all = gt_ref[...].reshape(nc * MLSTM_GATE_W, LANES)
    ls = _log_sigmoid(g_all)
    lane = lax.broadcasted_iota(I32, (1, LANES), 1)
    pre = ls
    suf = ls
    d = 1
    while d < LANES:
        pre = pre + jnp.where(lane >= d, pltpu.roll(pre, d, axis=1), 0.0)
        suf = suf + jnp.where(lane < LANES - d, pltpu.roll(suf, LANES - d, axis=1), 0.0)
        d *= 2
    ls_ref[...] = ls.reshape(nc, MLSTM_GATE_W, LANES)
    cs_ref[0] = pre.reshape(nc, MLSTM_GATE_W, LANES)
    cs_ref[1] = suf.reshape(nc, MLSTM_GATE_W, LANES)

    ct_ref[...] = jnp.zeros(ct_ref.shape, F32)
    m_ref[...] = jnp.zeros(m_ref.shape, F32)

    r_i = lax.broadcasted_iota(I32, (L, L), 0)
    c_i = lax.broadcasted_iota(I32, (L, L), 1)
    masks = (c_i <= r_i, c_i >= r_i)

    def chunk_step(dr, c, accumulate):
        mask = masks[dr]
        row_i = dr * 2 * H + hh
        row_f = dr * 2 * H + H + hh
        ig_row = gt_ref[c, pl.ds(row_i, 1), :]
        lf_row = ls_ref[c, pl.ds(row_f, 1), :]
        a_row = ig_row - cs_ref[dr, c, pl.ds(row_f, 1), :]
        am = jnp.where(mask, a_row, -jnp.inf)
        cmax = jnp.max(am, axis=1, keepdims=True)
        bcol = jnp.sum(jnp.where(mask, lf_row, 0.0), axis=1, keepdims=True)
        m_row = m_ref[dr, 0:1, :]
        mm = jnp.maximum(cmax, m_row)
        w = jnp.exp(am - mm)
        inter = jnp.exp(m_row - mm)
        clamp = jnp.exp(-(bcol + mm))
        row0 = pl.multiple_of(c * L, L)
        q_c = qs_ref[pl.ds(row0, L), :]
        k_c = ks_ref[pl.ds(row0, L), :]
        sw = (_dot_nt(q_c, k_c) * w).astype(BF16)
        qi = (q_c.astype(F32) * inter).astype(BF16)
        v1t = v1t_ref[c]
        ct = ct_ref[dr]
        lhs = jnp.concatenate([sw, qi], axis=1)
        rhs_t = jnp.concatenate([v1t, ct.astype(BF16)], axis=1)
        nd = _dot_nt(lhs, rhs_t)
        hval = nd[:, 0:L] / jnp.maximum(jnp.abs(nd[:, L:2 * L]), clamp)
        if accumulate:
            hs_ref[pl.ds(row0, L), :] = hs_ref[pl.ds(row0, L), :] + hval
        else:
            hs_ref[pl.ds(row0, L), :] = hval
        m_last = jnp.maximum(jnp.max(a_row, axis=1, keepdims=True), m_row)
        decay = jnp.exp(m_row - m_last)
        ws_row = jnp.exp(a_row - m_last)
        ut = _dot((v1t.astype(F32) * ws_row).astype(BF16), k_c)
        ct_ref[dr] = decay * ct + ut
        m_new = jnp.sum(lf_row, axis=1, keepdims=True) + m_last
        m_ref[dr] = jnp.broadcast_to(m_new, (SUBLANES, LANES))

    def first_half(i, carry):
        chunk_step(0, i, False)
        chunk_step(1, nc - 1 - i, False)
        return carry

    def second_half(i, carry):
        chunk_step(0, i, True)
        chunk_step(1, nc - 1 - i, True)
        return carry

    lax.fori_loop(0, nc // 2, first_half, 0, unroll=2)
    lax.fori_loop(nc // 2, nc, second_half, 0, unroll=2)

    for c in range(nc):
        hsum = hs_ref[c * L:(c + 1) * L, :]
        y = hsum * lax.rsqrt(jnp.mean(hsum * hsum, axis=-1, keepdims=True) + RMS_EPS) * og_ref[0]
        og = jax.nn.sigmoid(o_ref[c * L:(c + 1) * L, :].astype(F32))
        ym_ref[c * L:(c + 1) * L, :] = (y * og).astype(BF16)


def _mlstm(P, VT, GT, conv_w, conv_b, out_gain, B, S):
    H = N_HEADS_MLSTM
    L = MLSTM_L
    nc = S // L
    qcol = P_QM // LANES
    kcol = P_KM // LANES
    ocol = P_OM // LANES
    return pl.pallas_call(
        _mlstm_kernel,
        grid=(B, H),
        in_specs=[
            pl.BlockSpec((S, LANES), lambda b, h: (b, qcol + h)),
            pl.BlockSpec((S, LANES), lambda b, h: (b, kcol + h)),
            pl.BlockSpec((nc, LANES, LANES), lambda b, h: (b, h, 0)),
            pl.BlockSpec((S, LANES), lambda b, h: (b, ocol + h)),
            pl.BlockSpec((nc, MLSTM_GATE_W, LANES), lambda b, h: (b, 0, 0)),
            pl.BlockSpec((CONV_WIDTH, LANES), lambda b, h: (0, h)),
            pl.BlockSpec((CONV_WIDTH, LANES), lambda b, h: (0, H + h)),
            pl.BlockSpec((1, LANES), lambda b, h: (0, h)),
            pl.BlockSpec((1, LANES), lambda b, h: (0, H + h)),
            pl.BlockSpec((1, 1, LANES), lambda b, h: (h, 0, 0)),
        ],
        out_specs=pl.BlockSpec((S, LANES), lambda b, h: (b, h)),
        out_shape=jax.ShapeDtypeStruct((B * S, MLSTM_W), BF16),
        scratch_shapes=[
            pltpu.VMEM((S + 2 * SUBLANES, LANES), F32),
            pltpu.VMEM((S, LANES), BF16),
            pltpu.VMEM((S, LANES), BF16),
            pltpu.VMEM((nc, 2 * L, LANES), BF16),
            pltpu.VMEM((nc, MLSTM_GATE_W, LANES), F32),
            pltpu.VMEM((2, nc, MLSTM_GATE_W, LANES), F32),
            pltpu.VMEM((2, 2 * L, LANES), F32),
            pltpu.VMEM((2, SUBLANES, LANES), F32),
            pltpu.VMEM((S, LANES), F32),
        ],
        compiler_params=pltpu.CompilerParams(
            dimension_semantics=("arbitrary", "arbitrary"), vmem_limit_bytes=VMEM_LIMIT_BYTES),
        name="mlstm",
    )(P, P, VT, P, GT, conv_w, conv_w, conv_b, conv_b, out_gain[:, None, :])


def _merge_kernel(ya_ref, ym_ref, mga_ref, mgm_ref, mb_ref, x_ref, wa_ref, wm_ref, wo_ref, g2_ref,
                  wrh_ref, wrl_ref, br_ref, tri_ref, x1_ref, h2_ref, eid_ref, wt_ref, cnt_ref, run_ref):
    tm = x_ref.shape[0]

    @pl.when(pl.program_id(0) == 0)
    def _():
        run_ref[...] = jnp.zeros(run_ref.shape, F32)
    a = _dot(ya_ref[...], wa_ref[...])
    mm = _dot(ym_ref[...], wm_ref[...])
    ga = jax.nn.sigmoid(mga_ref[...].astype(F32) + mb_ref[:, 0:D_MODEL])
    gm = jax.nn.sigmoid(mgm_ref[...].astype(F32) + mb_ref[:, D_MODEL:2 * D_MODEL])
    u = (ga * a + gm * mm).astype(BF16)
    x1 = x_ref[...] + _dot(u, wo_ref[...])
    x1_ref[...] = x1
    h2 = x1 * lax.rsqrt(jnp.mean(x1 * x1, axis=-1, keepdims=True) + RMS_EPS) * g2_ref[...]
    for j in range(D_MODEL // LANES):
        h2_ref[pl.ds(j, tm, stride=ROW_TILE), :] = h2[:, j * LANES:(j + 1) * LANES]

    hi = h2.astype(BF16)
    lo = (h2 - hi.astype(F32)).astype(BF16)
    lg = (_dot_nt(wrh_ref[...], hi) + _dot_nt(wrh_ref[...], lo) + _dot_nt(wrl_ref[...], hi)
          + br_ref[...])
    row = lax.broadcasted_iota(I32, (SUBLANES, tm), 0)
    gl = jnp.where(row < N_GROUPS, lg[0:SUBLANES], -jnp.inf)
    gmax = jnp.max(gl, axis=0, keepdims=True)
    g_idx = jnp.min(jnp.where(gl == gmax, row, SUBLANES), axis=0, keepdims=True)
    p_group = 1.0 / jnp.sum(jnp.exp(gl - gmax), axis=0, keepdims=True)
    e_in = lg[SUBLANES * N_GROUPS:SUBLANES * (N_GROUPS + 1)]
    for g in range(N_GROUPS - 2, -1, -1):
        e_in = jnp.where(g_idx == g, lg[SUBLANES * (g + 1):SUBLANES * (g + 2)], e_in)
    v1 = jnp.max(e_in, axis=0, keepdims=True)
    i1 = jnp.min(jnp.where(e_in == v1, row, SUBLANES), axis=0, keepdims=True)
    e2 = jnp.where(row == i1, -jnp.inf, e_in)
    v2 = jnp.max(e2, axis=0, keepdims=True)
    i2 = jnp.min(jnp.where(e2 == v2, row, SUBLANES), axis=0, keepdims=True)
    t = jnp.exp(v2 - v1)
    w1 = p_group / (1.0 + t)
    w2 = p_group * t / (1.0 + t)
    id1 = g_idx * EXPERTS_PER_GROUP + i1
    id2 = g_idx * EXPERTS_PER_GROUP + i2
    wt_ref[...] = jnp.where(row == 0, w1, jnp.where(row == 1, w2, 0.0))

    erow = lax.broadcasted_iota(I32, (N_EXPERTS, tm), 0)
    oh1 = (erow == id1).astype(F32)
    oh2 = (erow == id2).astype(F32)
    c1 = _dot(oh1.astype(BF16), tri_ref[...])
    c2 = _dot(oh2.astype(BF16), tri_ref[...])
    tot1 = jnp.sum(oh1, axis=1, keepdims=True)
    tot2 = jnp.sum(oh2, axis=1, keepdims=True)
    base = run_ref[:, 0:1]
    r1 = jnp.sum(oh1 * (base + c1), axis=0, keepdims=True)
    r2 = jnp.sum(oh2 * (base + tot1 + c2), axis=0, keepdims=True)
    run_ref[...] = run_ref[...] + (tot1 + tot2)
    cnt_ref[...] = run_ref[...]
    eid_ref[...] = jnp.where(row == 0, id1, jnp.where(row == 1, id2, jnp.where(
        row == 2, r1.astype(I32), jnp.where(row == 3, r2.astype(I32), 0))))


def _merge(ya, ym, P, merge_b, x2, w_a, w_m, w_o, g2, wr_hi, wr_lo, b_r):
    T = x2.shape[0]
    tm = TOK_TILE
    const = lambda i: (0, 0)
    ti = jnp.arange(tm, dtype=I32)
    tri = (ti[:, None] < ti[None, :]).astype(BF16)
    return pl.pallas_call(
        _merge_kernel,
        grid=(T // tm,),
        in_specs=[
            pl.BlockSpec((tm, ATTN_Q_W), lambda i: (i, 0)),
            pl.BlockSpec((tm, MLSTM_W), lambda i: (i, 0)),
            pl.BlockSpec((tm, D_MODEL), lambda i: (i, 0)),
            pl.BlockSpec((tm, D_MODEL), lambda i: (i, 1)),
            pl.BlockSpec((1, 2 * D_MODEL), const),
            pl.BlockSpec((tm, D_MODEL), lambda i: (i, 0)),
            pl.BlockSpec((ATTN_Q_W, D_MODEL), const),
            pl.BlockSpec((MLSTM_W, D_MODEL), const),
            pl.BlockSpec((D_MODEL, D_MODEL), const),
            pl.BlockSpec((1, D_MODEL), const),
            pl.BlockSpec((LANES, D_MODEL), const),
            pl.BlockSpec((LANES, D_MODEL), const),
            pl.BlockSpec((LANES, 1), const),
            pl.BlockSpec((tm, tm), const),
        ],
        out_specs=[
            pl.BlockSpec((tm, D_MODEL), lambda i: (i, 0)),
            pl.BlockSpec((tm * ROW_TILE, LANES), lambda i: (i, 0)),
            pl.BlockSpec((SUBLANES, tm), lambda i: (0, i)),
            pl.BlockSpec((SUBLANES, tm), lambda i: (0, i)),
            pl.BlockSpec((N_EXPERTS, LANES), const),
        ],
        out_shape=[
            jax.ShapeDtypeStruct((T, D_MODEL), F32),
            jax.ShapeDtypeStruct((T * ROW_TILE, LANES), F32),
            jax.ShapeDtypeStruct((SUBLANES, T), I32),
            jax.ShapeDtypeStruct((SUBLANES, T), F32),
            jax.ShapeDtypeStruct((N_EXPERTS, LANES), F32),
        ],
        scratch_shapes=[pltpu.VMEM((N_EXPERTS, LANES), F32)],
        compiler_params=pltpu.CompilerParams(
            dimension_semantics=("arbitrary",), vmem_limit_bytes=VMEM_LIMIT_BYTES),
        name="merge",
    )(ya, ym, P, P, merge_b, x2, w_a, w_m, w_o, g2, wr_hi, wr_lo, b_r, tri)


def _inv_kernel(dest_ref, pad_hbm, inv_hbm, inv_smem, sem, *, chunk):
    c = pl.program_id(0)

    @pl.when(c == 0)
    def _():
        cp = pltpu.make_async_copy(pad_hbm, inv_smem, sem)
        cp.start()
        cp.wait()

    def body(i, carry):
        inv_smem[dest_ref[0, 0, i]] = c * chunk + i
        return carry
    lax.fori_loop(0, chunk, body, 0, unroll=16)

    @pl.when(c == pl.num_programs(0) - 1)
    def _():
        cp = pltpu.make_async_copy(inv_smem, inv_hbm, sem)
        cp.start()
        cp.wait()


def _inverse_map(dest_flat, n_rows, pad_base):
    A = dest_flat.shape[0]
    chunk = min(A, 4096)
    nch = A // chunk
    pad_ids = pad_base + (jnp.arange(n_rows, dtype=I32) & (2 * MOE_BLK - 1))
    return pl.pallas_call(
        functools.partial(_inv_kernel, chunk=chunk),
        grid=(nch,),
        in_specs=[
            pl.BlockSpec((1, 1, chunk), lambda c: (c, 0, 0), memory_space=pltpu.SMEM),
            pl.BlockSpec(memory_space=pl.ANY),
        ],
        out_specs=pl.BlockSpec(memory_space=pl.ANY),
        scratch_shapes=[pltpu.SMEM((n_rows,), I32), pltpu.SemaphoreType.DMA],
        out_shape=jax.ShapeDtypeStruct((n_rows,), I32),
        compiler_params=pltpu.CompilerParams(dimension_semantics=("arbitrary",)),
        name="inverse_map",
    )(dest_flat.reshape(nch, 1, chunk), pad_ids)


def _moe_kernel(blk_e_ref, nused_ref,
                invp_ref, invc_ref, invn_ref, h2_hbm, wg_ref, wu_ref, wd_ref,
                y_hbm,
                xg_ref, ys_ref, xs_ref, wgb_ref, wub_ref, wdb_ref, gsem, ssem, *, n_tok):
    blk = MOE_BLK
    b = pl.program_id(0)
    nused = nused_ref[0]
    slot = lax.rem(b, 2)
    oslot = 1 - slot
    tok_mask = n_tok - 1
    n_lane_tiles = D_MODEL // LANES

    def gather_copy(r, a, slot_):
        return pltpu.make_async_copy(
            h2_hbm.at[pl.ds((a & tok_mask) * ROW_TILE, ROW_TILE), :],
            xg_ref.at[slot_, pl.ds(r * ROW_TILE, ROW_TILE), :],
            gsem.at[slot_])

    def scatter_copy(r, a, slot_):
        return pltpu.make_async_copy(
            ys_ref.at[slot_, pl.ds(r * ROW_TILE, ROW_TILE), :],
            y_hbm.at[pl.ds(a * ROW_TILE, ROW_TILE), :],
            ssem.at[slot_])

    def wait_rows(copy_fn, slot_):
        for _ in range(blk):
            copy_fn(0, 0, slot_).wait()

    @pl.when(b < nused)
    def _():
        @pl.when(b == 0)
        def _():
            ys_ref[...] = jnp.zeros(ys_ref.shape, F32)

            def prologue(r, carry):
                scatter_copy(r, TOP_K * n_tok + r, 0).start()
                gather_copy(r, invc_ref[0, 0, r], 0).start()
                return carry
            lax.fori_loop(0, blk, prologue, 0)

        changed = (b == 0) | (blk_e_ref[b] != blk_e_ref[jnp.maximum(b - 1, 0)])

        @pl.when(changed)
        def _():
            wgb_ref[...] = wg_ref[0].astype(BF16)
            wub_ref[...] = wu_ref[0].astype(BF16)
            wdb_ref[...] = wd_ref[0].astype(BF16)

        wait_rows(gather_copy, slot)
        per = blk // n_lane_tiles
        for j in range(n_lane_tiles):
            xs_ref[:, j * LANES:(j + 1) * LANES] = (
                xg_ref[slot, pl.ds(j, blk, stride=ROW_TILE), :].astype(BF16))
            for r in range(j * per, (j + 1) * per):
                gather_copy(r, invn_ref[0, 0, r], oslot).start(priority=r % 2)
        x = xs_ref[...]
        nw = 256
        acts = []
        r0 = 0
        n_up = D_EXPERT // nw
        n_dn = D_MODEL // nw
        per_up = blk // (2 * n_up)
        per_dn = (blk - n_up * per_up) // n_dn
        for n in range(n_up):
            g = _dot(x, wgb_ref[:, n * nw:(n + 1) * nw])
            u = _dot(x, wub_ref[:, n * nw:(n + 1) * nw])
            acts.append((g * jax.nn.sigmoid(g) * u).astype(BF16))
            for r in range(r0, r0 + per_up):
                scatter_copy(r, invp_ref[0, 0, r], oslot).start(priority=r % 2)
            r0 += per_up
        act = jnp.concatenate(acts, axis=1)
        wait_rows(scatter_copy, slot)
        for n in range(n_dn):
            y = _dot(act, wdb_ref[:, n * nw:(n + 1) * nw])
            for jj in range(nw // LANES):
                j = n * (nw // LANES) + jj
                ys_ref[slot, pl.ds(j, blk, stride=ROW_TILE), :] = y[:, jj * LANES:(jj + 1) * LANES]
            for r in range(r0, r0 + per_dn):
                scatter_copy(r, invp_ref[0, 0, r], oslot).start(priority=r % 2)
            r0 += per_dn
        assert r0 == blk

        @pl.when(b == nused - 1)
        def _():
            def epilogue(r, carry):
                scatter_copy(r, invc_ref[0, 0, r], slot).start()
                return carry
            lax.fori_loop(0, blk, epilogue, 0)
            wait_rows(scatter_copy, oslot)
            wait_rows(scatter_copy, slot)
            wait_rows(gather_copy, oslot)


def _moe(blk_e, nused, inv, h2g, w_gate, w_up, w_down, T):
    blk = MOE_BLK
    assert T & (T - 1) == 0, "token count must be a power of two for the id -> token mask"
    nblk = inv.shape[0] // blk
    inv3 = inv.reshape(nblk, 1, blk)
    virt = (TOP_K * T + blk + jnp.arange(blk, dtype=I32)).reshape(1, 1, blk)
    invp3 = jnp.concatenate([virt, inv3[:-1]], axis=0)
    grid_spec = pltpu.PrefetchScalarGridSpec(
        num_scalar_prefetch=2,
        grid=(nblk,),
        in_specs=[
            pl.BlockSpec((1, 1, blk), lambda b, e, n: (b, 0, 0), memory_space=pltpu.SMEM),
            pl.BlockSpec((1, 1, blk), lambda b, e, n: (b, 0, 0), memory_space=pltpu.SMEM),
            pl.BlockSpec((1, 1, blk), lambda b, e, n: (jnp.minimum(b + 1, nblk - 1), 0, 0),
                         memory_space=pltpu.SMEM),
            pl.BlockSpec(memory_space=pl.ANY),
            pl.BlockSpec((1, D_MODEL, D_EXPERT), lambda b, e, n: (e[b], 0, 0)),
            pl.BlockSpec((1, D_MODEL, D_EXPERT), lambda b, e, n: (e[b], 0, 0)),
            pl.BlockSpec((1, D_EXPERT, D_MODEL), lambda b, e, n: (e[b], 0, 0)),
        ],
        out_specs=pl.BlockSpec(memory_space=pl.ANY),
        scratch_shapes=[
            pltpu.VMEM((2, blk * ROW_TILE, LANES), F32),
            pltpu.VMEM((2, blk * ROW_TILE, LANES), F32),
            pltpu.VMEM((blk, D_MODEL), BF16),
            pltpu.VMEM((D_MODEL, D_EXPERT), BF16),
            pltpu.VMEM((D_MODEL, D_EXPERT), BF16),
            pltpu.VMEM((D_EXPERT, D_MODEL), BF16),
            pltpu.SemaphoreType.DMA((2,)),
            pltpu.SemaphoreType.DMA((2,)),
        ],
    )
    return pl.pallas_call(
        functools.partial(_moe_kernel, n_tok=T),
        grid_spec=grid_spec,
        out_shape=jax.ShapeDtypeStruct(((TOP_K * T + 2 * blk) * ROW_TILE, LANES), F32),
        compiler_params=pltpu.CompilerParams(
            dimension_semantics=("arbitrary",), vmem_limit_bytes=VMEM_LIMIT_BYTES),
        name="moe",
    )(blk_e, nused, invp3, inv3, inv3, h2g, w_gate, w_up, w_down)


def _combine_kernel(x1_ref, y0_ref, y1_ref, wt_ref, o_ref):
    tm = x1_ref.shape[0]
    r_i = lax.broadcasted_iota(I32, (LANES, LANES), 0)
    c_i = lax.broadcasted_iota(I32, (LANES, LANES), 1)
    eye = r_i == c_i
    for c in range(tm // LANES):
        rows = pl.ds(c * LANES, LANES)
        w0 = jnp.sum(jnp.where(eye, wt_ref[0:1, c * LANES:(c + 1) * LANES], 0.0), axis=1, keepdims=True)
        w1 = jnp.sum(jnp.where(eye, wt_ref[1:2, c * LANES:(c + 1) * LANES], 0.0), axis=1, keepdims=True)
        for j in range(D_MODEL // LANES):
            sl = slice(j * LANES, (j + 1) * LANES)
            tile = pl.ds(c * LANES * ROW_TILE + j, LANES, stride=ROW_TILE)
            o_ref[rows, sl] = x1_ref[rows, sl] + w0 * y0_ref[tile, :] + w1 * y1_ref[tile, :]


def _combine(x1, Y, wts):
    T = x1.shape[0]
    tm = TOK_TILE
    nt = T // tm
    return pl.pallas_call(
        _combine_kernel,
        grid=(nt,),
        in_specs=[
            pl.BlockSpec((tm, D_MODEL), lambda i: (i, 0)),
            pl.BlockSpec((tm * ROW_TILE, LANES), lambda i: (i, 0)),
            pl.BlockSpec((tm * ROW_TILE, LANES), lambda i: (i + nt, 0)),
            pl.BlockSpec((SUBLANES, tm), lambda i: (0, i)),
        ],
        out_specs=pl.BlockSpec((tm, D_MODEL), lambda i: (i, 0)),
        out_shape=jax.ShapeDtypeStruct((T, D_MODEL), F32),
        compiler_params=pltpu.CompilerParams(
            dimension_semantics=("arbitrary",), vmem_limit_bytes=VMEM_LIMIT_BYTES),
        name="combine",
    )(x1, Y, Y, wts)


def _t5_bucket(rel):
    half = N_REL_BUCKETS // 2
    max_exact = half // 2
    sign = jnp.where(rel > 0, half, 0)
    n = jnp.abs(rel)
    nf = jnp.maximum(n, 1).astype(jnp.float32)
    large = max_exact + (jnp.log(nf / max_exact) / math.log(REL_MAX_DISTANCE / max_exact)
                         * (half - max_exact)).astype(jnp.int32)
    large = jnp.minimum(large, half - 1)
    return sign + jnp.where(n < max_exact, n, large)


def _attn_bias(rel_table):
    qi = jnp.arange(ATTN_BLOCK)
    ks = jnp.arange(3 * ATTN_BLOCK) - ATTN_BLOCK
    rel = ks[None, :] - qi[:, None]
    onehot = jax.nn.one_hot(_t5_bucket(rel), N_REL_BUCKETS, dtype=F32)
    bias = jnp.einsum('qsb,bh->hqs', onehot, rel_table.astype(F32),
                      precision=lax.Precision.HIGHEST)
    bias = jnp.where((jnp.abs(rel) <= WINDOW)[None], bias, NEG_INF)
    not_prev = (ks >= 0)[None, None, :]
    not_next = (ks < ATTN_BLOCK)[None, None, :]
    return jnp.stack([bias,
                      jnp.where(not_prev, bias, NEG_INF),
                      jnp.where(not_next, bias, NEG_INF),
                      jnp.where(not_prev & not_next, bias, NEG_INF)], axis=0)


def _block_table(counts, T):
    blk = MOE_BLK
    nblk = (TOP_K * T) // blk + N_EXPERTS
    pcounts = ((counts + blk - 1) // blk) * blk
    pends = jnp.cumsum(pcounts)
    pstarts = (pends - pcounts).astype(I32)
    nused = (pends[-1] // blk).astype(I32)
    first_row = jnp.arange(nblk, dtype=I32) * blk
    blk_e = jnp.sum((first_row[:, None] >= pends[None, :]).astype(I32), axis=1)
    blk_e = jnp.minimum(blk_e, N_EXPERTS - 1)
    last_e = jnp.sum((((nused - 1) * blk) >= pends).astype(I32))
    blk_e = jnp.where(jnp.arange(nblk) < nused, blk_e, jnp.minimum(last_e, N_EXPERTS - 1)).astype(I32)
    return pstarts, blk_e, nused.reshape(1), nblk * blk


def kernel(x, norm1_g, w_in, attn_q_gain, attn_k_gain, attn_sink, rel_bias_table, mlstm_conv_w,
           mlstm_conv_b, mlstm_gate_b, mlstm_out_gain, w_branch_attn, w_branch_mlstm, merge_b, w_out,
           norm2_g, w_router_group, b_router_group, w_router_expert, b_router_expert,
           w_expert_gate, w_expert_up, w_expert_down):
    B, S, D = x.shape
    T = B * S
    depth = norm1_g.shape[0]
    x2 = x.reshape(T, D)
    bias = _attn_bias(rel_bias_table)
    for l in range(depth):
        w = w_in[l]
        w_main = jnp.concatenate([w[:, _OFF_MG:_W_IN], w[:, _OFF_QA:_OFF_VM], w[:, _OFF_OM:_OFF_GM]],
                                 axis=1).astype(BF16)
        w_vt = w[:, _OFF_VM:_OFF_OM].T.astype(BF16)
        w_gt = w[:, _OFF_GM:_OFF_MG].T.astype(BF16)
        gate_b = mlstm_gate_b[l].reshape(MLSTM_GATE_W, 1).astype(F32)
        P, VT, GT = _in_proj(x2, norm1_g[l][None, :], w_main, w_vt, w_gt, gate_b)

        qg = jnp.tile(attn_q_gain[l], LANES // HEAD_DIM_ATTN)[None, :]
        kg = jnp.tile(attn_k_gain[l], LANES // HEAD_DIM_ATTN)[None, :]
        ya = _attention(P, bias, attn_sink[l].astype(F32), qg, kg, B, S)

        ym = _mlstm(P, VT, GT, mlstm_conv_w[l], mlstm_conv_b[l][None, :], mlstm_out_gain[l], B, S)

        wr = jnp.zeros((LANES, D), F32)
        wr = wr.at[0:N_GROUPS].set(w_router_group[l].T)
        wr = wr.at[SUBLANES:SUBLANES + N_EXPERTS].set(w_router_expert[l].T)
        wr_hi = wr.astype(BF16)
        wr_lo = (wr - wr_hi.astype(F32)).astype(BF16)
        b_r = jnp.zeros((LANES,), F32)
        b_r = b_r.at[0:N_GROUPS].set(b_router_group[l])
        b_r = b_r.at[SUBLANES:SUBLANES + N_EXPERTS].set(b_router_expert[l])
        x1, h2g, route, wts, cnt = _merge(
            ya, ym, P, merge_b[l][None, :], x2, w_branch_attn[l].astype(BF16),
            w_branch_mlstm[l].astype(BF16), w_out[l].astype(BF16), norm2_g[l][None, :],
            wr_hi, wr_lo, b_r[:, None])

        pstarts, blk_e, nused, n_rows = _block_table(cnt[:, 0].astype(I32), T)
        eid_flat = route[0:TOP_K].reshape(TOP_K * T)
        seg_start = jnp.sum(jnp.where(eid_flat[None, :] == jnp.arange(N_EXPERTS, dtype=I32)[:, None],
                                      pstarts[:, None], 0), axis=0)
        dest = seg_start + route[TOP_K:2 * TOP_K].reshape(TOP_K * T)
        inv = _inverse_map(dest, n_rows, TOP_K * T)
        Y = _moe(blk_e, nused, inv, h2g, w_expert_gate[l], w_expert_up[l], w_expert_down[l], T)
        x2 = _combine(x1, Y, wts)
    return x2.reshape(B, S, D)
```

```python
import functools
import math

import jax
import jax.numpy as jnp
from jax import lax
from jax.experimental import pallas as pl
from jax.experimental.pallas import tpu as pltpu

F32 = jnp.float32
BF16 = jnp.bfloat16
I32 = jnp.int32

LANES = 128
SUBLANES = 8
VMEM_LIMIT_BYTES = 56 * 1024 * 1024

D_MODEL = 1024
N_HEADS_ATTN = 8
N_KV_HEADS = 2
HEAD_DIM_ATTN = 64
WINDOW = 128
ATTN_BLOCK = WINDOW
N_REL_BUCKETS = 32
REL_MAX_DISTANCE = 128
N_HEADS_MLSTM = 4
HEAD_DIM_MLSTM = 128
CONV_WIDTH = 5
ATTN_Q_W = N_HEADS_ATTN * HEAD_DIM_ATTN
ATTN_KV_W = N_KV_HEADS * HEAD_DIM_ATTN
MLSTM_W = N_HEADS_MLSTM * HEAD_DIM_MLSTM
MLSTM_GATE_W = 2 * 2 * N_HEADS_MLSTM
N_GROUPS = 4
EXPERTS_PER_GROUP = 8
N_EXPERTS = N_GROUPS * EXPERTS_PER_GROUP
TOP_K = 2
D_EXPERT = 512
RMS_EPS = 1e-6
NEG_INF = -1e30

_OFF_QA = 0
_OFF_VM = ATTN_Q_W + 2 * ATTN_KV_W + 2 * MLSTM_W
_OFF_OM = _OFF_VM + MLSTM_W
_OFF_GM = _OFF_OM + MLSTM_W
_OFF_MG = _OFF_GM + MLSTM_GATE_W
_W_IN = _OFF_MG + 2 * D_MODEL

P_MG = 0
P_QA = 2 * D_MODEL
P_KA = P_QA + ATTN_Q_W
P_VA = P_KA + ATTN_KV_W
P_QM = P_VA + ATTN_KV_W
P_KM = P_QM + MLSTM_W
P_OM = P_KM + MLSTM_W
P_W = P_OM + MLSTM_W

ATTN_ROWS = 32
MLSTM_L = 128
TOK_TILE = 512
MOE_BLK = 256
ROW_TILE = SUBLANES

_NT = (((1,), (1,)), ((), ()))


def _dot(a, b):
    return jnp.dot(a, b, preferred_element_type=F32)


def _dot_nt(a, b):
    return lax.dot_general(a, b, _NT, preferred_element_type=F32)


def _inproj_kernel(x_ref, g1_ref, wm_ref, wvt_ref, wgt_ref, gb_ref, p_ref, vt_ref, gt_ref):
    x = x_ref[...]
    h = x * lax.rsqrt(jnp.mean(x * x, axis=-1, keepdims=True) + RMS_EPS) * g1_ref[...]
    hb = h.astype(BF16)
    step = 512
    for c0 in range(0, P_W, step):
        w = min(step, P_W - c0)
        p_ref[:, c0:c0 + w] = _dot(hb, wm_ref[:, c0:c0 + w]).astype(BF16)
    vt = _dot_nt(wvt_ref[...], hb)
    gt = _dot_nt(wgt_ref[...], hb) + gb_ref[...]
    for cc in range(x.shape[0] // LANES):
        vt_ref[cc] = vt[:, cc * LANES:(cc + 1) * LANES].astype(BF16)
        gt_ref[cc] = gt[:, cc * LANES:(cc + 1) * LANES]


def _in_proj(x2, g1, w_main, w_vt, w_gt, gate_b):
    T = x2.shape[0]
    tm = TOK_TILE
    nck = tm // LANES
    return pl.pallas_call(
        _inproj_kernel,
        grid=(T // tm,),
        in_specs=[
            pl.BlockSpec((tm, D_MODEL), lambda i: (i, 0)),
            pl.BlockSpec((1, D_MODEL), lambda i: (0, 0)),
            pl.BlockSpec((D_MODEL, P_W), lambda i: (0, 0)),
            pl.BlockSpec((MLSTM_W, D_MODEL), lambda i: (0, 0)),
            pl.BlockSpec((MLSTM_GATE_W, D_MODEL), lambda i: (0, 0)),
            pl.BlockSpec((MLSTM_GATE_W, 1), lambda i: (0, 0)),
        ],
        out_specs=[
            pl.BlockSpec((tm, P_W), lambda i: (i, 0)),
            pl.BlockSpec((nck, MLSTM_W, LANES), lambda i: (i, 0, 0)),
            pl.BlockSpec((nck, MLSTM_GATE_W, LANES), lambda i: (i, 0, 0)),
        ],
        out_shape=[
            jax.ShapeDtypeStruct((T, P_W), BF16),
            jax.ShapeDtypeStruct((T // LANES, MLSTM_W, LANES), BF16),
            jax.ShapeDtypeStruct((T // LANES, MLSTM_GATE_W, LANES), F32),
        ],
        compiler_params=pltpu.CompilerParams(
            dimension_semantics=("arbitrary",), vmem_limit_bytes=VMEM_LIMIT_BYTES),
        name="in_proj",
    )(x2, g1, w_main, w_vt, w_gt, gate_b)


def _attn_kernel(sink_ref, q_ref, kp_ref, kc_ref, kn_ref, vp_ref, vc_ref, vn_ref,
                 bias_ref, qg_ref, kg_ref, o_ref):
    half = HEAD_DIM_ATTN
    Q = ATTN_BLOCK
    n_tiles = ATTN_Q_W // LANES
    group = N_HEADS_ATTN // N_KV_HEADS
    r_i = lax.broadcasted_iota(I32, (LANES, LANES), 0)
    c_i = lax.broadcasted_iota(I32, (LANES, LANES), 1)
    bd = ((r_i // half) == (c_i // half)).astype(BF16)
    lane = lax.broadcasted_iota(I32, (1, LANES), 1)
    lo_mask = lane < half

    parts = [kp_ref[...], kc_ref[...], kn_ref[...]] + [q_ref[:, j * LANES:(j + 1) * LANES] for j in range(n_tiles)]
    t = jnp.concatenate(parts, axis=0).astype(F32)
    tt = t * t
    hi = tt.astype(BF16)
    lo = (tt - hi.astype(F32)).astype(BF16)
    nrow = t.shape[0]
    ms2 = _dot(jnp.concatenate([hi, lo], axis=0), bd)
    ms = (ms2[0:nrow] + ms2[nrow:2 * nrow]) * (1.0 / half)
    tn = t * lax.rsqrt(ms + RMS_EPS)
    kn = tn[0:3 * Q] * kg_ref[...]
    k_nat = kn.astype(BF16)
    k_rot = pltpu.roll(kn, half, axis=1).astype(BF16)
    qn = [tn[3 * Q + j * Q:3 * Q + (j + 1) * Q] * qg_ref[...] * (HEAD_DIM_ATTN ** -0.5) for j in range(n_tiles)]

    v = jnp.concatenate([vp_ref[...], vc_ref[...], vn_ref[...]], axis=0).astype(F32)
    v_rot = pltpu.roll(v, half, axis=1)
    zero = jnp.zeros_like(v)
    v_pair = [
        jnp.concatenate([jnp.where(lo_mask, v, zero), jnp.where(lo_mask, zero, v_rot)], axis=0).astype(BF16),
        jnp.concatenate([jnp.where(lo_mask, v_rot, zero), jnp.where(lo_mask, zero, v)], axis=0).astype(BF16),
    ]

    def q_masked(hd):
        keep = lo_mask if hd % 2 == 0 else ~lo_mask
        return jnp.where(keep, qn[hd // 2], 0.0).astype(BF16)

    heads_nat = [hd for hd in range(N_HEADS_ATTN) if (hd // group) == (hd % 2)]
    heads_rot = [hd for hd in range(N_HEADS_ATTN) if (hd // group) != (hd % 2)]
    scores = {}
    for kx, hds in ((k_nat, heads_nat), (k_rot, heads_rot)):
        s_all = _dot_nt(jnp.concatenate([q_masked(hd) for hd in hds], axis=0), kx)
        for i, hd in enumerate(hds):
            scores[hd] = s_all[i * Q:(i + 1) * Q]

    for j in range(n_tiles):
        probs = []
        for hd in (2 * j, 2 * j + 1):
            s = scores[hd] + bias_ref[0, hd]
            sink = sink_ref[hd]
            m = jnp.maximum(jnp.max(s, axis=1, keepdims=True), sink)
            e = jnp.exp(s - m)
            den = jnp.sum(e, axis=1, keepdims=True) + jnp.exp(sink - m)
            probs.append((e * (1.0 / den)).astype(BF16))
        kv = (2 * j) // group
        o_ref[:, j * LANES:(j + 1) * LANES] = _dot(jnp.concatenate(probs, axis=1), v_pair[kv]).astype(BF16)


def _attention(P, bias, sink, qg, kg, B, S):
    nb = S // ATTN_BLOCK
    Q = ATTN_BLOCK
    qcol = P_QA // ATTN_Q_W
    kcol = P_KA // ATTN_KV_W
    vcol = P_VA // ATTN_KV_W

    def blk(col, shift):
        def imap(b, n, sink_ref):
            nn = jnp.clip(n + shift, 0, nb - 1)
            return (b * nb + nn, col)
        return pl.BlockSpec((Q, ATTN_KV_W), imap)

    grid_spec = pltpu.PrefetchScalarGridSpec(
        num_scalar_prefetch=1,
        grid=(B, nb),
        in_specs=[
            pl.BlockSpec((Q, ATTN_Q_W), lambda b, n, s: (b * nb + n, qcol)),
            blk(kcol, -1), blk(kcol, 0), blk(kcol, 1),
            blk(vcol, -1), blk(vcol, 0), blk(vcol, 1),
            pl.BlockSpec((1, N_HEADS_ATTN, Q, 3 * Q),
                         lambda b, n, s: ((n == 0).astype(I32) + 2 * (n == nb - 1).astype(I32), 0, 0, 0)),
            pl.BlockSpec((1, LANES), lambda b, n, s: (0, 0)),
            pl.BlockSpec((1, LANES), lambda b, n, s: (0, 0)),
        ],
        out_specs=pl.BlockSpec((Q, ATTN_Q_W), lambda b, n, s: (b * nb + n, 0)),
    )
    return pl.pallas_call(
        _attn_kernel,
        grid_spec=grid_spec,
        out_shape=jax.ShapeDtypeStruct((B * S, ATTN_Q_W), BF16),
        compiler_params=pltpu.CompilerParams(
            dimension_semantics=("arbitrary", "arbitrary"), vmem_limit_bytes=VMEM_LIMIT_BYTES),
        name="attn",
    )(sink, P, P, P, P, P, P, P, bias, qg, kg)


def _log_sigmoid(x):
    return jnp.minimum(x, 0.0) - jnp.log1p(jnp.exp(-jnp.abs(x)))


def _mlstm_kernel(q_ref, k_ref, vt_ref, o_ref, gt_ref, cwq_ref, cwk_ref, cbq_ref, cbk_ref, og_ref,
                  ym_ref,
                  pad_ref, qs_ref, ks_ref, v1t_ref, ls_ref, cs_ref, ct_ref, m_ref, hs_ref):
    L = MLSTM_L
    S = q_ref.shape[0]
    nc = S // L
    hh = pl.program_id(1)
    H = N_HEADS_MLSTM

    halo = SUBLANES
    zpad = jnp.zeros((halo, LANES), F32)
    pad_ref[0:halo, :] = zpad
    pad_ref[halo + S:2 * halo + S, :] = zpad

    def conv_silu(src_ref, w_ref, b_ref, dst_ref, scale):
        pad_ref[halo:halo + S, :] = src_ref[...].astype(F32)
        for c in range(nc):
            acc = jnp.broadcast_to(b_ref[...], (L, LANES))
            for j in range(CONV_WIDTH):
                off = c * L + halo - CONV_WIDTH // 2 + j
                acc = acc + pad_ref[off:off + L, :] * w_ref[j:j + 1, :]
            y = acc * jax.nn.sigmoid(acc)
            dst_ref[c * L:(c + 1) * L, :] = (y * scale).astype(BF16)

    conv_silu(q_ref, cwq_ref, cbq_ref, qs_ref, 1.0)
    conv_silu(k_ref, cwk_ref, cbk_ref, ks_ref, HEAD_DIM_MLSTM ** -0.5)

    ones_blk = jnp.ones((L, LANES), BF16)
    for c in range(nc):
        v1t_ref[c, 0:L, :] = vt_ref[c]
        v1t_ref[c, L:2 * L, :] = ones_blk

    g_all = gt_ref[...].reshape(nc * MLSTM_GATE_W, LANES)
    ls = _log_sigmoid(g_all)
    lane = lax.broadcasted_iota(I32, (1, LANES), 1)
    pre = ls
    suf = ls
    d = 1
    while d < LANES:
        pre = pre + jnp.where(lane >= d, pltpu.roll(pre, d, axis=1), 0.0)
        suf = suf + jnp.where(lane < LANES - d, pltpu.roll(suf, LANES - d, axis=1), 0.0)
        d *= 2
    ls_ref[...] = ls.reshape(nc, MLSTM_GATE_W, LANES)
    cs_ref[0] = pre.reshape(nc, MLSTM_GATE_W, LANES)
    cs_ref[1] = suf.reshape(nc, MLSTM_GATE_W, LANES)

    ct_ref[...] = jnp.zeros(ct_ref.shape, F32)
    m_ref[...] = jnp.zeros(m_ref.shape, F32)

    r_i = lax.broadcasted_iota(I32, (L, L), 0)
    c_i = lax.broadcasted_iota(I32, (L, L), 1)
    masks = (c_i <= r_i, c_i >= r_i)

    def chunk_step(dr, c, accumulate):
        mask = masks[dr]
        row_i = dr * 2 * H + hh
        row_f = dr * 2 * H + H + hh
        ig_row = gt_ref[c, pl.ds(row_i, 1), :]
        lf_row = ls_ref[c, pl.ds(row_f, 1), :]
        a_row = ig_row - cs_ref[dr, c, pl.ds(row_f, 1), :]
        am = jnp.where(mask, a_row, -jnp.inf)
        cmax = jnp.max(am, axis=1, keepdims=True)
        bcol = jnp.sum(jnp.where(mask, lf_row, 0.0), axis=1, keepdims=True)
        m_row = m_ref[dr, 0:1, :]
        mm = jnp.maximum(cmax, m_row)
        w = jnp.exp(am - mm)
        inter = jnp.exp(m_row - mm)
        clamp = jnp.exp(-(bcol + mm))
        row0 = pl.multiple_of(c * L, L)
        q_c = qs_ref[pl.ds(row0, L), :]
        k_c = ks_ref[pl.ds(row0, L), :]
        sw = (_dot_nt(q_c, k_c) * w).astype(BF16)
        qi = (q_c.astype(F32) * inter).astype(BF16)
        v1t = v1t_ref[c]
        ct = ct_ref[dr]
        lhs = jnp.concatenate([sw, qi], axis=1)
        rhs_t = jnp.concatenate([v1t, ct.astype(BF16)], axis=1)
        nd = _dot_nt(lhs, rhs_t)
        hval = nd[:, 0:L] / jnp.maximum(jnp.abs(nd[:, L:2 * L]), clamp)
        if accumulate:
            hs_ref[pl.ds(row0, L), :] = hs_ref[pl.ds(row0, L), :] + hval
        else:
            hs_ref[pl.ds(row0, L), :] = hval
        m_last = jnp.maximum(jnp.max(a_row, axis=1, keepdims=True), m_row)
        decay = jnp.exp(m_row - m_last)
        ws_row = jnp.exp(a_row - m_last)
        ut = _dot((v1t.astype(F32) * ws_row).astype(BF16), k_c)
        ct_ref[dr] = decay * ct + ut
        m_new = jnp.sum(lf_row, axis=1, keepdims=True) + m_last
        m_ref[dr] = jnp.broadcast_to(m_new, (SUBLANES, LANES))

    def first_half(i, carry):
        chunk_step(0, i, False)
        chunk_step(1, nc - 1 - i, False)
        return carry

    def second_half(i, carry):
        chunk_step(0, i, True)
        chunk_step(1, nc - 1 - i, True)
        return carry

    lax.fori_loop(0, nc // 2, first_half, 0, unroll=True)
    lax.fori_loop(nc // 2, nc, second_half, 0, unroll=True)

    for c in range(nc):
        hsum = hs_ref[c * L:(c + 1) * L, :]
        y = hsum * lax.rsqrt(jnp.mean(hsum * hsum, axis=-1, keepdims=True) + RMS_EPS) * og_ref[0]
        og = jax.nn.sigmoid(o_ref[c * L:(c + 1) * L, :].astype(F32))
        ym_ref[c * L:(c + 1) * L, :] = (y * og).astype(BF16)


def _mlstm(P, VT, GT, conv_w, conv_b, out_gain, B, S):
    H = N_HEADS_MLSTM
    L = MLSTM_L
    nc = S // L
    qcol = P_QM // LANES
    kcol = P_KM // LANES
    ocol = P_OM // LANES
    return pl.pallas_call(
        _mlstm_kernel,
        grid=(B, H),
        in_specs=[
            pl.BlockSpec((S, LANES), lambda b, h: (b, qcol + h)),
            pl.BlockSpec((S, LANES), lambda b, h: (b, kcol + h)),
            pl.BlockSpec((nc, LANES, LANES), lambda b, h: (b, h, 0)),
            pl.BlockSpec((S, LANES), lambda b, h: (b, ocol + h)),
            pl.BlockSpec((nc, MLSTM_GATE_W, LANES), lambda b, h: (b, 0, 0)),
            pl.BlockSpec((CONV_WIDTH, LANES), lambda b, h: (0, h)),
            pl.BlockSpec((CONV_WIDTH, LANES), lambda b, h: (0, H + h)),
            pl.BlockSpec((1, LANES), lambda b, h: (0, h)),
            pl.BlockSpec((1, LANES), lambda b, h: (0, H + h)),
            pl.BlockSpec((1, 1, LANES), lambda b, h: (h, 0, 0)),
        ],
        out_specs=pl.BlockSpec((S, LANES), lambda b, h: (b, h)),
        out_shape=jax.ShapeDtypeStruct((B * S, MLSTM_W), BF16),
        scratch_shapes=[
            pltpu.VMEM((S + 2 * SUBLANES, LANES), F32),
            pltpu.VMEM((S, LANES), BF16),
            pltpu.VMEM((S, LANES), BF16),
            pltpu.VMEM((nc, 2 * L, LANES), BF16),
            pltpu.VMEM((nc, MLSTM_GATE_W, LANES), F32),
            pltpu.VMEM((2, nc, MLSTM_GATE_W, LANES), F32),
            pltpu.VMEM((2, 2 * L, LANES), F32),
            pltpu.VMEM((2, SUBLANES, LANES), F32),
            pltpu.VMEM((S, LANES), F32),
        ],
        compiler_params=pltpu.CompilerParams(
            dimension_semantics=("arbitrary", "arbitrary"), vmem_limit_bytes=VMEM_LIMIT_BYTES),
        name="mlstm",
    )(P, P, VT, P, GT, conv_w, conv_w, conv_b, conv_b, out_gain[:, None, :])


def _merge_kernel(ya_ref, ym_ref, mga_ref, mgm_ref, mb_ref, x_ref, wa_ref, wm_ref, wo_ref, g2_ref,
                  wrh_ref, wrl_ref, br_ref, tri_ref, x1_ref, h2_ref, eid_ref, wt_ref, cnt_ref, run_ref):
    tm = x_ref.shape[0]

    @pl.when(pl.program_id(0) == 0)
    def _():
        run_ref[...] = jnp.zeros(run_ref.shape, F32)
    a = _dot(ya_ref[...], wa_ref[...])
    mm = _dot(ym_ref[...], wm_ref[...])
    ga = jax.nn.sigmoid(mga_ref[...].astype(F32) + mb_ref[:, 0:D_MODEL])
    gm = jax.nn.sigmoid(mgm_ref[...].astype(F32) + mb_ref[:, D_MODEL:2 * D_MODEL])
    u = (ga * a + gm * mm).astype(BF16)
    x1 = x_ref[...] + _dot(u, wo_ref[...])
    x1_ref[...] = x1
    h2 = x1 * lax.rsqrt(jnp.mean(x1 * x1, axis=-1, keepdims=True) + RMS_EPS) * g2_ref[...]
    for j in range(D_MODEL // LANES):
        h2_ref[pl.ds(j, tm, stride=ROW_TILE), :] = h2[:, j * LANES:(j + 1) * LANES]

    hi = h2.astype(BF16)
    lo = (h2 - hi.astype(F32)).astype(BF16)
    by_hi = _dot_nt(jnp.concatenate([wrh_ref[...], wrl_ref[...]], axis=0), hi)
    lg = by_hi[0:LANES] + by_hi[LANES:2 * LANES] + _dot_nt(wrh_ref[...], lo) + br_ref[...]
    row = lax.broadcasted_iota(I32, (SUBLANES, tm), 0)
    gl = jnp.where(row < N_GROUPS, lg[0:SUBLANES], -jnp.inf)
    gmax = jnp.max(gl, axis=0, keepdims=True)
    g_idx = jnp.min(jnp.where(gl == gmax, row, SUBLANES), axis=0, keepdims=True)
    p_group = 1.0 / jnp.sum(jnp.exp(gl - gmax), axis=0, keepdims=True)
    e_in = lg[SUBLANES * N_GROUPS:SUBLANES * (N_GROUPS + 1)]
    for g in range(N_GROUPS - 2, -1, -1):
        e_in = jnp.where(g_idx == g, lg[SUBLANES * (g + 1):SUBLANES * (g + 2)], e_in)
    v1 = jnp.max(e_in, axis=0, keepdims=True)
    i1 = jnp.min(jnp.where(e_in == v1, row, SUBLANES), axis=0, keepdims=True)
    e2 = jnp.where(row == i1, -jnp.inf, e_in)
    v2 = jnp.max(e2, axis=0, keepdims=True)
    i2 = jnp.min(jnp.where(e2 == v2, row, SUBLANES), axis=0, keepdims=True)
    t = jnp.exp(v2 - v1)
    w1 = p_group / (1.0 + t)
    w2 = p_group * t / (1.0 + t)
    id1 = g_idx * EXPERTS_PER_GROUP + i1
    id2 = g_idx * EXPERTS_PER_GROUP + i2
    wt_ref[...] = jnp.where(row == 0, w1, jnp.where(row == 1, w2, 0.0))

    erow = lax.broadcasted_iota(I32, (N_EXPERTS, tm), 0)
    oh1 = (erow == id1).astype(F32)
    oh2 = (erow == id2).astype(F32)
    c1 = _dot(oh1.astype(BF16), tri_ref[...])
    c2 = _dot(oh2.astype(BF16), tri_ref[...])
    tot1 = jnp.sum(oh1, axis=1, keepdims=True)
    tot2 = jnp.sum(oh2, axis=1, keepdims=True)
    base = run_ref[:, 0:1]
    r1 = jnp.sum(oh1 * (base + c1), axis=0, keepdims=True)
    r2 = jnp.sum(oh2 * (base + tot1 + c2), axis=0, keepdims=True)
    run_ref[...] = run_ref[...] + (tot1 + tot2)
    cnt_ref[...] = run_ref[...]
    eid_ref[...] = jnp.where(row == 0, id1, jnp.where(row == 1, id2, jnp.where(
        row == 2, r1.astype(I32), jnp.where(row == 3, r2.astype(I32), 0))))


def _merge(ya, ym, P, merge_b, x2, w_a, w_m, w_o, g2, wr_hi, wr_lo, b_r):
    T = x2.shape[0]
    tm = TOK_TILE
    const = lambda i: (0, 0)
    ti = jnp.arange(tm, dtype=I32)
    tri = (ti[:, None] < ti[None, :]).astype(BF16)
    return pl.pallas_call(
        _merge_kernel,
        grid=(T // tm,),
        in_specs=[
            pl.BlockSpec((tm, ATTN_Q_W), lambda i: (i, 0)),
            pl.BlockSpec((tm, MLSTM_W), lambda i: (i, 0)),
            pl.BlockSpec((tm, D_MODEL), lambda i: (i, 0)),
            pl.BlockSpec((tm, D_MODEL), lambda i: (i, 1)),
            pl.BlockSpec((1, 2 * D_MODEL), const),
            pl.BlockSpec((tm, D_MODEL), lambda i: (i, 0)),
            pl.BlockSpec((ATTN_Q_W, D_MODEL), const),
            pl.BlockSpec((MLSTM_W, D_MODEL), const),
            pl.BlockSpec((D_MODEL, D_MODEL), const),
            pl.BlockSpec((1, D_MODEL), const),
            pl.BlockSpec((LANES, D_MODEL), const),
            pl.BlockSpec((LANES, D_MODEL), const),
            pl.BlockSpec((LANES, 1), const),
            pl.BlockSpec((tm, tm), const),
        ],
        out_specs=[
            pl.BlockSpec((tm, D_MODEL), lambda i: (i, 0)),
            pl.BlockSpec((tm * ROW_TILE, LANES), lambda i: (i, 0)),
            pl.BlockSpec((SUBLANES, tm), lambda i: (0, i)),
            pl.BlockSpec((SUBLANES, tm), lambda i: (0, i)),
            pl.BlockSpec((N_EXPERTS, LANES), const),
        ],
        out_shape=[
            jax.ShapeDtypeStruct((T, D_MODEL), F32),
            jax.ShapeDtypeStruct((T * ROW_TILE, LANES), F32),
            jax.ShapeDtypeStruct((SUBLANES, T), I32),
            jax.ShapeDtypeStruct((SUBLANES, T), F32),
            jax.ShapeDtypeStruct((N_EXPERTS, LANES), F32),
        ],
        scratch_shapes=[pltpu.VMEM((N_EXPERTS, LANES), F32)],
        compiler_params=pltpu.CompilerParams(
            dimension_semantics=("arbitrary",), vmem_limit_bytes=VMEM_LIMIT_BYTES),
        name="merge",
    )(ya, ym, P, P, merge_b, x2, w_a, w_m, w_o, g2, wr_hi, wr_lo, b_r, tri)


def _inv_kernel(dest_ref, pad_hbm, inv_hbm, inv_smem, sem, *, chunk):
    c = pl.program_id(0)

    @pl.when(c == 0)
    def _():
        cp = pltpu.make_async_copy(pad_hbm, inv_smem, sem)
        cp.start()
        cp.wait()

    def body(i, carry):
        inv_smem[dest_ref[0, 0, i]] = c * chunk + i
        return carry
    lax.fori_loop(0, chunk, body, 0, unroll=16)

    @pl.when(c == pl.num_programs(0) - 1)
    def _():
        cp = pltpu.make_async_copy(inv_smem, inv_hbm, sem)
        cp.start()
        cp.wait()


def _inverse_map(dest_flat, n_rows, pad_base):
    A = dest_flat.shape[0]
    chunk = min(A, 4096)
    nch = A // chunk
    pad_ids = pad_base + (jnp.arange(n_rows, dtype=I32) & (2 * MOE_BLK - 1))
    return pl.pallas_call(
        functools.partial(_inv_kernel, chunk=chunk),
        grid=(nch,),
        in_specs=[
            pl.BlockSpec((1, 1, chunk), lambda c: (c, 0, 0), memory_space=pltpu.SMEM),
            pl.BlockSpec(memory_space=pl.ANY),
        ],
        out_specs=pl.BlockSpec(memory_space=pl.ANY),
        scratch_shapes=[pltpu.SMEM((n_rows,), I32), pltpu.SemaphoreType.DMA],
        out_shape=jax.ShapeDtypeStruct((n_rows,), I32),
        compiler_params=pltpu.CompilerParams(dimension_semantics=("arbitrary",)),
        name="inverse_map",
    )(dest_flat.reshape(nch, 1, chunk), pad_ids)


def _moe_kernel(blk_e_ref, nused_ref,
                invp_ref, invc_ref, invn_ref, h2_hbm, wg_ref, wu_ref, wd_ref,
                y_hbm,
                xg_ref, ys_ref, xs_ref, wgb_ref, wub_ref, wdb_ref, gsem, ssem, *, n_tok):
    blk = MOE_BLK
    b = pl.program_id(0)
    nused = nused_ref[0]
    slot = lax.rem(b, 2)
    oslot = 1 - slot
    tok_mask = n_tok - 1
    n_lane_tiles = D_MODEL // LANES

    def gather_copy(r, a, slot_):
        return pltpu.make_async_copy(
            h2_hbm.at[pl.ds((a & tok_mask) * ROW_TILE, ROW_TILE), :],
            xg_ref.at[slot_, pl.ds(r * ROW_TILE, ROW_TILE), :],
            gsem.at[slot_])

    def scatter_copy(r, a, slot_):
        return pltpu.make_async_copy(
            ys_ref.at[slot_, pl.ds(r * ROW_TILE, ROW_TILE), :],
            y_hbm.at[pl.ds(a * ROW_TILE, ROW_TILE), :],
            ssem.at[slot_])

    def wait_rows(copy_fn, slot_):
        for _ in range(blk):
            copy_fn(0, 0, slot_).wait()

    @pl.when(b < nused)
    def _():
        @pl.when(b == 0)
        def _():
            ys_ref[...] = jnp.zeros(ys_ref.shape, F32)

            def prologue(r, carry):
                scatter_copy(r, TOP_K * n_tok + r, 0).start()
                gather_copy(r, invc_ref[0, 0, r], 0).start()
                return carry
            lax.fori_loop(0, blk, prologue, 0)

        changed = (b == 0) | (blk_e_ref[b] != blk_e_ref[jnp.maximum(b - 1, 0)])

        @pl.when(changed)
        def _():
            wgb_ref[...] = wg_ref[0].astype(BF16)
            wub_ref[...] = wu_ref[0].astype(BF16)
            wdb_ref[...] = wd_ref[0].astype(BF16)

        wait_rows(gather_copy, slot)
        per = blk // n_lane_tiles
        for j in range(n_lane_tiles):
            xs_ref[:, j * LANES:(j + 1) * LANES] = (
                xg_ref[slot, pl.ds(j, blk, stride=ROW_TILE), :].astype(BF16))
            for r in range(j * per, (j + 1) * per):
                gather_copy(r, invn_ref[0, 0, r], oslot).start(priority=r % 2)
        x = xs_ref[...]
        nw = 256
        acts = []
        r0 = 0
        n_up = D_EXPERT // nw
        n_dn = D_MODEL // nw
        per_up = blk // (2 * n_up)
        per_dn = (blk - n_up * per_up) // n_dn
        for n in range(n_up):
            g = _dot(x, wgb_ref[:, n * nw:(n + 1) * nw])
            u = _dot(x, wub_ref[:, n * nw:(n + 1) * nw])
            acts.append((g * jax.nn.sigmoid(g) * u).astype(BF16))
            for r in range(r0, r0 + per_up):
                scatter_copy(r, invp_ref[0, 0, r], oslot).start(priority=r % 2)
            r0 += per_up
        act = jnp.concatenate(acts, axis=1)
        wait_rows(scatter_copy, slot)
        for n in range(n_dn):
            y = _dot(act, wdb_ref[:, n * nw:(n + 1) * nw])
            for jj in range(nw // LANES):
                j = n * (nw // LANES) + jj
                ys_ref[slot, pl.ds(j, blk, stride=ROW_TILE), :] = y[:, jj * LANES:(jj + 1) * LANES]
            for r in range(r0, r0 + per_dn):
                scatter_copy(r, invp_ref[0, 0, r], oslot).start(priority=r % 2)
            r0 += per_dn
        assert r0 == blk

        @pl.when(b == nused - 1)
        def _():
            def epilogue(r, carry):
                scatter_copy(r, invc_ref[0, 0, r], slot).start()
                return carry
            lax.fori_loop(0, blk, epilogue, 0)
            wait_rows(scatter_copy, oslot)
            wait_rows(scatter_copy, slot)
            wait_rows(gather_copy, oslot)


def _moe(blk_e, nused, inv, h2g, w_gate, w_up, w_down, T):
    blk = MOE_BLK
    assert T & (T - 1) == 0, "token count must be a power of two for the id -> token mask"
    nblk = inv.shape[0] // blk
    inv3 = inv.reshape(nblk, 1, blk)
    virt = (TOP_K * T + blk + jnp.arange(blk, dtype=I32)).reshape(1, 1, blk)
    invp3 = jnp.concatenate([virt, inv3[:-1]], axis=0)
    grid_spec = pltpu.PrefetchScalarGridSpec(
        num_scalar_prefetch=2,
        grid=(nblk,),
        in_specs=[
            pl.BlockSpec((1, 1, blk), lambda b, e, n: (b, 0, 0), memory_space=pltpu.SMEM),
            pl.BlockSpec((1, 1, blk), lambda b, e, n: (b, 0, 0), memory_space=pltpu.SMEM),
            pl.BlockSpec((1, 1, blk), lambda b, e, n: (jnp.minimum(b + 1, nblk - 1), 0, 0),
                         memory_space=pltpu.SMEM),
            pl.BlockSpec(memory_space=pl.ANY),
            pl.BlockSpec((1, D_MODEL, D_EXPERT), lambda b, e, n: (e[b], 0, 0)),
            pl.BlockSpec((1, D_MODEL, D_EXPERT), lambda b, e, n: (e[b], 0, 0)),
            pl.BlockSpec((1, D_EXPERT, D_MODEL), lambda b, e, n: (e[b], 0, 0)),
        ],
        out_specs=pl.BlockSpec(memory_space=pl.ANY),
        scratch_shapes=[
            pltpu.VMEM((2, blk * ROW_TILE, LANES), F32),
            pltpu.VMEM((2, blk * ROW_TILE, LANES), F32),
            pltpu.VMEM((blk, D_MODEL), BF16),
            pltpu.VMEM((D_MODEL, D_EXPERT), BF16),
            pltpu.VMEM((D_MODEL, D_EXPERT), BF16),
            pltpu.VMEM((D_EXPERT, D_MODEL), BF16),
            pltpu.SemaphoreType.DMA((2,)),
            pltpu.SemaphoreType.DMA((2,)),
        ],
    )
    return pl.pallas_call(
        functools.partial(_moe_kernel, n_tok=T),
        grid_spec=grid_spec,
        out_shape=jax.ShapeDtypeStruct(((TOP_K * T + 2 * blk) * ROW_TILE, LANES), F32),
        compiler_params=pltpu.CompilerParams(
            dimension_semantics=("arbitrary",), vmem_limit_bytes=VMEM_LIMIT_BYTES),
        name="moe",
    )(blk_e, nused, invp3, inv3, inv3, h2g, w_gate, w_up, w_down)


def _combine_kernel(x1_ref, y0_ref, y1_ref, wt_ref, o_ref):
    tm = x1_ref.shape[0]
    r_i = lax.broadcasted_iota(I32, (LANES, LANES), 0)
    c_i = lax.broadcasted_iota(I32, (LANES, LANES), 1)
    eye = r_i == c_i
    for c in range(tm // LANES):
        rows = pl.ds(c * LANES, LANES)
        w0 = jnp.sum(jnp.where(eye, wt_ref[0:1, c * LANES:(c + 1) * LANES], 0.0), axis=1, keepdims=True)
        w1 = jnp.sum(jnp.where(eye, wt_ref[1:2, c * LANES:(c + 1) * LANES], 0.0), axis=1, keepdims=True)
        for j in range(D_MODEL // LANES):
            sl = slice(j * LANES, (j + 1) * LANES)
            tile = pl.ds(c * LANES * ROW_TILE + j, LANES, stride=ROW_TILE)
            o_ref[rows, sl] = x1_ref[rows, sl] + w0 * y0_ref[tile, :] + w1 * y1_ref[tile, :]


def _combine(x1, Y, wts):
    T = x1.shape[0]
    tm = TOK_TILE
    nt = T // tm
    return pl.pallas_call(
        _combine_kernel,
        grid=(nt,),
        in_specs=[
            pl.BlockSpec((tm, D_MODEL), lambda i: (i, 0)),
            pl.BlockSpec((tm * ROW_TILE, LANES), lambda i: (i, 0)),
            pl.BlockSpec((tm * ROW_TILE, LANES), lambda i: (i + nt, 0)),
            pl.BlockSpec((SUBLANES, tm), lambda i: (0, i)),
        ],
        out_specs=pl.BlockSpec((tm, D_MODEL), lambda i: (i, 0)),
        out_shape=jax.ShapeDtypeStruct((T, D_MODEL), F32),
        compiler_params=pltpu.CompilerParams(
            dimension_semantics=("arbitrary",), vmem_limit_bytes=VMEM_LIMIT_BYTES),
        name="combine",
    )(x1, Y, Y, wts)


def _t5_bucket(rel):
    half = N_REL_BUCKETS // 2
    max_exact = half // 2
    sign = jnp.where(rel > 0, half, 0)
    n = jnp.abs(rel)
    nf = jnp.maximum(n, 1).astype(jnp.float32)
    large = max_exact + (jnp.log(nf / max_exact) / math.log(REL_MAX_DISTANCE / max_exact)
                         * (half - max_exact)).astype(jnp.int32)
    large = jnp.minimum(large, half - 1)
    return sign + jnp.where(n < max_exact, n, large)


def _attn_bias(rel_table):
    qi = jnp.arange(ATTN_BLOCK)
    ks = jnp.arange(3 * ATTN_BLOCK) - ATTN_BLOCK
    rel = ks[None, :] - qi[:, None]
    onehot = jax.nn.one_hot(_t5_bucket(rel), N_REL_BUCKETS, dtype=F32)
    bias = jnp.einsum('qsb,bh->hqs', onehot, rel_table.astype(F32),
                      precision=lax.Precision.HIGHEST)
    bias = jnp.where((jnp.abs(rel) <= WINDOW)[None], bias, NEG_INF)
    not_prev = (ks >= 0)[None, None, :]
    not_next = (ks < ATTN_BLOCK)[None, None, :]
    return jnp.stack([bias,
                      jnp.where(not_prev, bias, NEG_INF),
                      jnp.where(not_next, bias, NEG_INF),
                      jnp.where(not_prev & not_next, bias, NEG_INF)], axis=0)


def _block_table(counts, T):
    blk = MOE_BLK
    nblk = (TOP_K * T) // blk + N_EXPERTS
    pcounts = ((counts + blk - 1) // blk) * blk
    pends = jnp.cumsum(pcounts)
    pstarts = (pends - pcounts).astype(I32)
    nused = (pends[-1] // blk).astype(I32)
    first_row = jnp.arange(nblk, dtype=I32) * blk
    blk_e = jnp.sum((first_row[:, None] >= pends[None, :]).astype(I32), axis=1)
    blk_e = jnp.minimum(blk_e, N_EXPERTS - 1)
    last_e = jnp.sum((((nused - 1) * blk) >= pends).astype(I32))
    blk_e = jnp.where(jnp.arange(nblk) < nused, blk_e, jnp.minimum(last_e, N_EXPERTS - 1)).astype(I32)
    return pstarts, blk_e, nused.reshape(1), nblk * blk


def kernel(x, norm1_g, w_in, attn_q_gain, attn_k_gain, attn_sink, rel_bias_table, mlstm_conv_w,
           mlstm_conv_b, mlstm_gate_b, mlstm_out_gain, w_branch_attn, w_branch_mlstm, merge_b, w_out,
           norm2_g, w_router_group, b_router_group, w_router_expert, b_router_expert,
           w_expert_gate, w_expert_up, w_expert_down):
    B, S, D = x.shape
    T = B * S
    depth = norm1_g.shape[0]
    x2 = x.reshape(T, D)
    bias = _attn_bias(rel_bias_table)
    for l in range(depth):
        w = w_in[l]
        w_main = jnp.concatenate([w[:, _OFF_MG:_W_IN], w[:, _OFF_QA:_OFF_VM], w[:, _OFF_OM:_OFF_GM]],
                                 axis=1).astype(BF16)
        w_vt = w[:, _OFF_VM:_OFF_OM].T.astype(BF16)
        w_gt = w[:, _OFF_GM:_OFF_MG].T.astype(BF16)
        gate_b = mlstm_gate_b[l].reshape(MLSTM_GATE_W, 1).astype(F32)
        P, VT, GT = _in_proj(x2, norm1_g[l][None, :], w_main, w_vt, w_gt, gate_b)

        qg = jnp.tile(attn_q_gain[l], LANES // HEAD_DIM_ATTN)[None, :]
        kg = jnp.tile(attn_k_gain[l], LANES // HEAD_DIM_ATTN)[None, :]
        ya = _attention(P, bias, attn_sink[l].astype(F32), qg, kg, B, S)

        ym = _mlstm(P, VT, GT, mlstm_conv_w[l], mlstm_conv_b[l][None, :], mlstm_out_gain[l], B, S)

        wr = jnp.zeros((LANES, D), F32)
        wr = wr.at[0:N_GROUPS].set(w_router_group[l].T)
        wr = wr.at[SUBLANES:SUBLANES + N_EXPERTS].set(w_router_expert[l].T)
        wr_hi = wr.astype(BF16)
        wr_lo = (wr - wr_hi.astype(F32)).astype(BF16)
        b_r = jnp.zeros((LANES,), F32)
        b_r = b_r.at[0:N_GROUPS].set(b_router_group[l])
        b_r = b_r.at[SUBLANES:SUBLANES + N_EXPERTS].set(b_router_expert[l])
        x1, h2g, route, wts, cnt = _merge(
            ya, ym, P, merge_b[l][None, :], x2, w_branch_attn[l].astype(BF16),
            w_branch_mlstm[l].astype(BF16), w_out[l].astype(BF16), norm2_g[l][None, :],
            wr_hi, wr_lo, b_r[:, None])

        pstarts, blk_e, nused, n_rows = _block_table(cnt[:, 0].astype(I32), T)
        eid_flat = route[0:TOP_K].reshape(TOP_K * T)
        seg_start = jnp.sum(jnp.where(eid_flat[None, :] == jnp.arange(N_EXPERTS, dtype=I32)[:, None],
                                      pstarts[:, None], 0), axis=0)
        dest = seg_start + route[TOP_K:2 * TOP_K].reshape(TOP_K * T)
        inv = _inverse_map(dest, n_rows, TOP_K * T)
        Y = _moe(blk_e, nused, inv, h2g, w_expert_gate[l], w_expert_up[l], w_expert_down[l], T)
        x2 = _combine(x1, Y, wts)
    return x2.reshape(B, S, D)
```

```python
import functools
import math

import jax
import jax.numpy as jnp
from jax import lax
from jax.experimental import pallas as pl
from jax.experimental.pallas import tpu as pltpu

F32 = jnp.float32
BF16 = jnp.bfloat16
I32 = jnp.int32

LANES = 128
SUBLANES = 8
VMEM_LIMIT_BYTES = 56 * 1024 * 1024

D_MODEL = 1024
N_HEADS_ATTN = 8
N_KV_HEADS = 2
HEAD_DIM_ATTN = 64
WINDOW = 128
ATTN_BLOCK = WINDOW
N_REL_BUCKETS = 32
REL_MAX_DISTANCE = 128
N_HEADS_MLSTM = 4
HEAD_DIM_MLSTM = 128
CONV_WIDTH = 5
ATTN_Q_W = N_HEADS_ATTN * HEAD_DIM_ATTN
ATTN_KV_W = N_KV_HEADS * HEAD_DIM_ATTN
MLSTM_W = N_HEADS_MLSTM * HEAD_DIM_MLSTM
MLSTM_GATE_W = 2 * 2 * N_HEADS_MLSTM
N_GROUPS = 4
EXPERTS_PER_GROUP = 8
N_EXPERTS = N_GROUPS * EXPERTS_PER_GROUP
TOP_K = 2
D_EXPERT = 512
RMS_EPS = 1e-6
NEG_INF = -1e30

_OFF_QA = 0
_OFF_VM = ATTN_Q_W + 2 * ATTN_KV_W + 2 * MLSTM_W
_OFF_OM = _OFF_VM + MLSTM_W
_OFF_GM = _OFF_OM + MLSTM_W
_OFF_MG = _OFF_GM + MLSTM_GATE_W
_W_IN = _OFF_MG + 2 * D_MODEL

P_MG = 0
P_QA = 2 * D_MODEL
P_KA = P_QA + ATTN_Q_W
P_VA = P_KA + ATTN_KV_W
P_QM = P_VA + ATTN_KV_W
P_KM = P_QM + MLSTM_W
P_OM = P_KM + MLSTM_W
P_W = P_OM + MLSTM_W

ATTN_ROWS = 32
MLSTM_L = 128
TOK_TILE = 512
MOE_BLK = 256
ROW_TILE = SUBLANES

_NT = (((1,), (1,)), ((), ()))


def _dot(a, b):
    return jnp.dot(a, b, preferred_element_type=F32)


def _dot_nt(a, b):
    return lax.dot_general(a, b, _NT, preferred_element_type=F32)


def _inproj_kernel(x_ref, g1_ref, wm_ref, wvt_ref, wgt_ref, gb_ref, p_ref, qko_ref, vt_ref, gt_ref):
    x = x_ref[...]
    h = x * lax.rsqrt(jnp.mean(x * x, axis=-1, keepdims=True) + RMS_EPS) * g1_ref[...]
    hb = h.astype(BF16)
    step = 512
    for c0 in range(0, P_QM, step):
        w = min(step, P_QM - c0)
        p_ref[:, c0:c0 + w] = _dot(hb, wm_ref[:, c0:c0 + w]).astype(BF16)
    for c0 in range(P_QM, P_W, step):
        res = _dot(hb, wm_ref[:, c0:c0 + step]).astype(BF16)
        for jj in range(step // LANES):
            qko_ref[(c0 - P_QM) // LANES + jj] = res[:, jj * LANES:(jj + 1) * LANES]
    vt = _dot_nt(wvt_ref[...], hb)
    gt = _dot_nt(wgt_ref[...], hb) + gb_ref[...]
    for cc in range(x.shape[0] // LANES):
        vt_ref[cc] = vt[:, cc * LANES:(cc + 1) * LANES].astype(BF16)
        gt_ref[cc] = gt[:, cc * LANES:(cc + 1) * LANES]


def _in_proj(x2, g1, w_main, w_vt, w_gt, gate_b):
    T = x2.shape[0]
    tm = TOK_TILE
    nck = tm // LANES
    return pl.pallas_call(
        _inproj_kernel,
        grid=(T // tm,),
        in_specs=[
            pl.BlockSpec((tm, D_MODEL), lambda i: (i, 0)),
            pl.BlockSpec((1, D_MODEL), lambda i: (0, 0)),
            pl.BlockSpec((D_MODEL, P_W), lambda i: (0, 0)),
            pl.BlockSpec((MLSTM_W, D_MODEL), lambda i: (0, 0)),
            pl.BlockSpec((MLSTM_GATE_W, D_MODEL), lambda i: (0, 0)),
            pl.BlockSpec((MLSTM_GATE_W, 1), lambda i: (0, 0)),
        ],
        out_specs=[
            pl.BlockSpec((tm, P_QM), lambda i: (i, 0)),
            pl.BlockSpec((3 * N_HEADS_MLSTM, tm, LANES), lambda i: (0, i, 0)),
            pl.BlockSpec((nck, MLSTM_W, LANES), lambda i: (i, 0, 0)),
            pl.BlockSpec((nck, MLSTM_GATE_W, LANES), lambda i: (i, 0, 0)),
        ],
        out_shape=[
            jax.ShapeDtypeStruct((T, P_QM), BF16),
            jax.ShapeDtypeStruct((3 * N_HEADS_MLSTM, T, LANES), BF16),
            jax.ShapeDtypeStruct((T // LANES, MLSTM_W, LANES), BF16),
            jax.ShapeDtypeStruct((T // LANES, MLSTM_GATE_W, LANES), F32),
        ],
        compiler_params=pltpu.CompilerParams(
            dimension_semantics=("arbitrary",), vmem_limit_bytes=VMEM_LIMIT_BYTES),
        name="in_proj",
    )(x2, g1, w_main, w_vt, w_gt, gate_b)


def _attn_kernel(sink_ref, q_ref, kp_ref, kc_ref, kn_ref, vp_ref, vc_ref, vn_ref,
                 bias_ref, qg_ref, kg_ref, o_ref):
    half = HEAD_DIM_ATTN
    Q = ATTN_BLOCK
    n_tiles = ATTN_Q_W // LANES
    group = N_HEADS_ATTN // N_KV_HEADS
    r_i = lax.broadcasted_iota(I32, (LANES, LANES), 0)
    c_i = lax.broadcasted_iota(I32, (LANES, LANES), 1)
    bd = ((r_i // half) == (c_i // half)).astype(BF16)
    lane = lax.broadcasted_iota(I32, (1, LANES), 1)
    lo_mask = lane < half

    parts = [kp_ref[...], kc_ref[...], kn_ref[...]] + [q_ref[:, j * LANES:(j + 1) * LANES] for j in range(n_tiles)]
    t = jnp.concatenate(parts, axis=0).astype(F32)
    tt = t * t
    hi = tt.astype(BF16)
    lo = (tt - hi.astype(F32)).astype(BF16)
    nrow = t.shape[0]
    ms2 = _dot(jnp.concatenate([hi, lo], axis=0), bd)
    ms = (ms2[0:nrow] + ms2[nrow:2 * nrow]) * (1.0 / half)
    tn = t * lax.rsqrt(ms + RMS_EPS)
    kn = tn[0:3 * Q] * kg_ref[...]
    k_nat = kn.astype(BF16)
    k_rot = pltpu.roll(kn, half, axis=1).astype(BF16)
    qn = [tn[3 * Q + j * Q:3 * Q + (j + 1) * Q] * qg_ref[...] * (HEAD_DIM_ATTN ** -0.5) for j in range(n_tiles)]

    v = jnp.concatenate([vp_ref[...], vc_ref[...], vn_ref[...]], axis=0).astype(F32)
    v_rot = pltpu.roll(v, half, axis=1)
    zero = jnp.zeros_like(v)
    v_pair = [
        jnp.concatenate([jnp.where(lo_mask, v, zero), jnp.where(lo_mask, zero, v_rot)], axis=0).astype(BF16),
        jnp.concatenate([jnp.where(lo_mask, v_rot, zero), jnp.where(lo_mask, zero, v)], axis=0).astype(BF16),
    ]

    def q_masked(hd):
        keep = lo_mask if hd % 2 == 0 else ~lo_mask
        return jnp.where(keep, qn[hd // 2], 0.0).astype(BF16)

    heads_nat = [hd for hd in range(N_HEADS_ATTN) if (hd // group) == (hd % 2)]
    heads_rot = [hd for hd in range(N_HEADS_ATTN) if (hd // group) != (hd % 2)]
    scores = {}
    for kx, hds in ((k_nat, heads_nat), (k_rot, heads_rot)):
        s_all = _dot_nt(jnp.concatenate([q_masked(hd) for hd in hds], axis=0), kx)
        for i, hd in enumerate(hds):
            scores[hd] = s_all[i * Q:(i + 1) * Q]

    for j in range(n_tiles):
        probs = []
        for hd in (2 * j, 2 * j + 1):
            s = scores[hd] + bias_ref[0, hd]
            sink = sink_ref[hd]
            m = jnp.maximum(jnp.max(s, axis=1, keepdims=True), sink)
            e = jnp.exp(s - m)
            den = jnp.sum(e, axis=1, keepdims=True) + jnp.exp(sink - m)
            probs.append((e * (1.0 / den)).astype(BF16))
        kv = (2 * j) // group
        o_ref[:, j * LANES:(j + 1) * LANES] = _dot(jnp.concatenate(probs, axis=1), v_pair[kv]).astype(BF16)


def _attention(P, bias, sink, qg, kg, B, S):
    nb = S // ATTN_BLOCK
    Q = ATTN_BLOCK
    qcol = P_QA // ATTN_Q_W
    kcol = P_KA // ATTN_KV_W
    vcol = P_VA // ATTN_KV_W

    def blk(col, shift):
        def imap(b, n, sink_ref):
            nn = jnp.clip(n + shift, 0, nb - 1)
            return (b * nb + nn, col)
        return pl.BlockSpec((Q, ATTN_KV_W), imap)

    grid_spec = pltpu.PrefetchScalarGridSpec(
        num_scalar_prefetch=1,
        grid=(B, nb),
        in_specs=[
            pl.BlockSpec((Q, ATTN_Q_W), lambda b, n, s: (b * nb + n, qcol)),
            blk(kcol, -1), blk(kcol, 0), blk(kcol, 1),
            blk(vcol, -1), blk(vcol, 0), blk(vcol, 1),
            pl.BlockSpec((1, N_HEADS_ATTN, Q, 3 * Q),
                         lambda b, n, s: ((n == 0).astype(I32) + 2 * (n == nb - 1).astype(I32), 0, 0, 0)),
            pl.BlockSpec((1, LANES), lambda b, n, s: (0, 0)),
            pl.BlockSpec((1, LANES), lambda b, n, s: (0, 0)),
        ],
        out_specs=pl.BlockSpec((Q, ATTN_Q_W), lambda b, n, s: (b * nb + n, 0)),
    )
    return pl.pallas_call(
        _attn_kernel,
        grid_spec=grid_spec,
        out_shape=jax.ShapeDtypeStruct((B * S, ATTN_Q_W), BF16),
        compiler_params=pltpu.CompilerParams(
            dimension_semantics=("arbitrary", "arbitrary"), vmem_limit_bytes=VMEM_LIMIT_BYTES),
        name="attn",
    )(sink, P, P, P, P, P, P, P, bias, qg, kg)


def _log_sigmoid(x):
    return jnp.minimum(x, 0.0) - jnp.log1p(jnp.exp(-jnp.abs(x)))


def _mlstm_kernel(q_ref, k_ref, vt_ref, o_ref, gt_ref, cwq_ref, cwk_ref, cbq_ref, cbk_ref, og_ref,
                  ym_ref,
                  pad_ref, qs_ref, ks_ref, v1t_ref, ls_ref, cs_ref, ct_ref, m_ref, hs_ref):
    L = MLSTM_L
    S = q_ref.shape[0]
    nc = S // L
    hh = pl.program_id(1)
    H = N_HEADS_MLSTM

    halo = SUBLANES
    zpad = jnp.zeros((halo, LANES), F32)
    pad_ref[0:halo, :] = zpad
    pad_ref[halo + S:2 * halo + S, :] = zpad

    def conv_silu(src_ref, w_ref, b_ref, dst_ref, scale):
        pad_ref[halo:halo + S, :] = src_ref[...].astype(F32)
        for c in range(nc):
            acc = jnp.broadcast_to(b_ref[...], (L, LANES))
            for j in range(CONV_WIDTH):
                off = c * L + halo - CONV_WIDTH // 2 + j
                acc = acc + pad_ref[off:off + L, :] * w_ref[j:j + 1, :]
            y = acc * jax.nn.sigmoid(acc)
            dst_ref[c * L:(c + 1) * L, :] = (y * scale).astype(BF16)

    conv_silu(q_ref, cwq_ref, cbq_ref, qs_ref, 1.0)
    conv_silu(k_ref, cwk_ref, cbk_ref, ks_ref, HEAD_DIM_MLSTM ** -0.5)

    ones_blk = jnp.ones((L, LANES), BF16)
    for c in range(nc):
        v1t_ref[c, 0:L, :] = vt_ref[c]
        v1t_ref[c, L:2 * L, :] = ones_blk

    g_all = gt_ref[...].reshape(nc * MLSTM_GATE_W, LANES)
    ls = _log_sigmoid(g_all)
    lane = lax.broadcasted_iota(I32, (1, LANES), 1)
    pre = ls
    suf = ls
    d = 1
    while d < LANES:
        pre = pre + jnp.where(lane >= d, pltpu.roll(pre, d, axis=1), 0.0)
        suf = suf + jnp.where(lane < LANES - d, pltpu.roll(suf, LANES - d, axis=1), 0.0)
        d *= 2
    ls_ref[...] = ls.reshape(nc, MLSTM_GATE_W, LANES)
    cs_ref[0] = pre.reshape(nc, MLSTM_GATE_W, LANES)
    cs_ref[1] = suf.reshape(nc, MLSTM_GATE_W, LANES)

    ct_ref[...] = jnp.zeros(ct_ref.shape, F32)
    m_ref[...] = jnp.zeros(m_ref.shape, F32)

    r_i = lax.broadcasted_iota(I32, (L, L), 0)
    c_i = lax.broadcasted_iota(I32, (L, L), 1)
    masks = (c_i <= r_i, c_i >= r_i)

    def chunk_step(dr, c, accumulate):
        mask = masks[dr]
        row_i = dr * 2 * H + hh
        row_f = dr * 2 * H + H + hh
        ig_row = gt_ref[c, pl.ds(row_i, 1), :]
        lf_row = ls_ref[c, pl.ds(row_f, 1), :]
        a_row = ig_row - cs_ref[dr, c, pl.ds(row_f, 1), :]
        am = jnp.where(mask, a_row, -jnp.inf)
        cmax = jnp.max(am, axis=1, keepdims=True)
        bcol = jnp.sum(jnp.where(mask, lf_row, 0.0), axis=1, keepdims=True)
        m_row = m_ref[dr, 0:1, :]
        mm = jnp.maximum(cmax, m_row)
        w = jnp.exp(am - mm)
        inter = jnp.exp(m_row - mm)
        clamp = jnp.exp(-(bcol + mm))
        row0 = pl.multiple_of(c * L, L)
        q_c = qs_ref[pl.ds(row0, L), :]
        k_c = ks_ref[pl.ds(row0, L), :]
        sw = (_dot_nt(q_c, k_c) * w).astype(BF16)
        qi = (q_c.astype(F32) * inter).astype(BF16)
        v1t = v1t_ref[c]
        ct = ct_ref[dr]
        lhs = jnp.concatenate([sw, qi], axis=1)
        rhs_t = jnp.concatenate([v1t, ct.astype(BF16)], axis=1)
        nd = _dot_nt(lhs, rhs_t)
        hval = nd[:, 0:L] / jnp.maximum(jnp.abs(nd[:, L:2 * L]), clamp)
        if accumulate:
            hs_ref[pl.ds(row0, L), :] = hs_ref[pl.ds(row0, L), :] + hval
        else:
            hs_ref[pl.ds(row0, L), :] = hval
        m_last = jnp.maximum(jnp.max(a_row, axis=1, keepdims=True), m_row)
        decay = jnp.exp(m_row - m_last)
        ws_row = jnp.exp(a_row - m_last)
        ut = _dot((v1t.astype(F32) * ws_row).astype(BF16), k_c)
        ct_ref[dr] = decay * ct + ut
        m_new = jnp.sum(lf_row, axis=1, keepdims=True) + m_last
        m_ref[dr] = jnp.broadcast_to(m_new, (SUBLANES, LANES))

    def first_half(i, carry):
        chunk_step(0, i, False)
        chunk_step(1, nc - 1 - i, False)
        return carry

    def second_half(i, carry):
        chunk_step(0, i, True)
        chunk_step(1, nc - 1 - i, True)
        return carry

    lax.fori_loop(0, nc // 2, first_half, 0, unroll=True)
    lax.fori_loop(nc // 2, nc, second_half, 0, unroll=True)

    for c in range(nc):
        hsum = hs_ref[c * L:(c + 1) * L, :]
        y = hsum * lax.rsqrt(jnp.mean(hsum * hsum, axis=-1, keepdims=True) + RMS_EPS) * og_ref[0]
        og = jax.nn.sigmoid(o_ref[c * L:(c + 1) * L, :].astype(F32))
        ym_ref[c * L:(c + 1) * L, :] = (y * og).astype(BF16)


def _mlstm(QKO, VT, GT, conv_w, conv_b, out_gain, B, S):
    H = N_HEADS_MLSTM
    L = MLSTM_L
    nc = S // L
    return pl.pallas_call(
        _mlstm_kernel,
        grid=(B, H),
        in_specs=[
            pl.BlockSpec((None, S, LANES), lambda b, h: (h, b, 0)),
            pl.BlockSpec((None, S, LANES), lambda b, h: (H + h, b, 0)),
            pl.BlockSpec((nc, LANES, LANES), lambda b, h: (b, h, 0)),
            pl.BlockSpec((None, S, LANES), lambda b, h: (2 * H + h, b, 0)),
            pl.BlockSpec((nc, MLSTM_GATE_W, LANES), lambda b, h: (b, 0, 0)),
            pl.BlockSpec((CONV_WIDTH, LANES), lambda b, h: (0, h)),
            pl.BlockSpec((CONV_WIDTH, LANES), lambda b, h: (0, H + h)),
            pl.BlockSpec((1, LANES), lambda b, h: (0, h)),
            pl.BlockSpec((1, LANES), lambda b, h: (0, H + h)),
            pl.BlockSpec((1, 1, LANES), lambda b, h: (h, 0, 0)),
        ],
        out_specs=pl.BlockSpec((S, LANES), lambda b, h: (b, h)),
        out_shape=jax.ShapeDtypeStruct((B * S, MLSTM_W), BF16),
        scratch_shapes=[
            pltpu.VMEM((S + 2 * SUBLANES, LANES), F32),
            pltpu.VMEM((S, LANES), BF16),
            pltpu.VMEM((S, LANES), BF16),
            pltpu.VMEM((nc, 2 * L, LANES), BF16),
            pltpu.VMEM((nc, MLSTM_GATE_W, LANES), F32),
            pltpu.VMEM((2, nc, MLSTM_GATE_W, LANES), F32),
            pltpu.VMEM((2, 2 * L, LANES), F32),
            pltpu.VMEM((2, SUBLANES, LANES), F32),
            pltpu.VMEM((S, LANES), F32),
        ],
        compiler_params=pltpu.CompilerParams(
            dimension_semantics=("arbitrary", "arbitrary"), vmem_limit_bytes=VMEM_LIMIT_BYTES),
        name="mlstm",
    )(QKO, QKO, VT, QKO, GT, conv_w, conv_w, conv_b, conv_b, out_gain[:, None, :])


def _merge_kernel(ya_ref, ym_ref, mga_ref, mgm_ref, mb_ref, x_ref, wa_ref, wm_ref, wo_ref, g2_ref,
                  wr_ref, br_ref, tri_ref, x1_ref, h2_ref, eid_ref, wt_ref, cnt_ref, run_ref):
    tm = x_ref.shape[0]

    @pl.when(pl.program_id(0) == 0)
    def _():
        run_ref[...] = jnp.zeros(run_ref.shape, F32)
    a = _dot(ya_ref[...], wa_ref[...])
    mm = _dot(ym_ref[...], wm_ref[...])
    ga = jax.nn.sigmoid(mga_ref[...].astype(F32) + mb_ref[:, 0:D_MODEL])
    gm = jax.nn.sigmoid(mgm_ref[...].astype(F32) + mb_ref[:, D_MODEL:2 * D_MODEL])
    u = (ga * a + gm * mm).astype(BF16)
    x1 = x_ref[...] + _dot(u, wo_ref[...])
    x1_ref[...] = x1
    h2 = x1 * lax.rsqrt(jnp.mean(x1 * x1, axis=-1, keepdims=True) + RMS_EPS) * g2_ref[...]
    for j in range(D_MODEL // LANES):
        h2_ref[pl.ds(j, tm, stride=ROW_TILE), :] = h2[:, j * LANES:(j + 1) * LANES]

    lg = _dot_nt(wr_ref[...], h2.astype(BF16)) + br_ref[...]
    row = lax.broadcasted_iota(I32, (SUBLANES, tm), 0)
    gl = jnp.where(row < N_GROUPS, lg[0:SUBLANES], -jnp.inf)
    gmax = jnp.max(gl, axis=0, keepdims=True)
    g_idx = jnp.min(jnp.where(gl == gmax, row, SUBLANES), axis=0, keepdims=True)
    p_group = 1.0 / jnp.sum(jnp.exp(gl - gmax), axis=0, keepdims=True)
    e_in = lg[SUBLANES * N_GROUPS:SUBLANES * (N_GROUPS + 1)]
    for g in range(N_GROUPS - 2, -1, -1):
        e_in = jnp.where(g_idx == g, lg[SUBLANES * (g + 1):SUBLANES * (g + 2)], e_in)
    v1 = jnp.max(e_in, axis=0, keepdims=True)
    i1 = jnp.min(jnp.where(e_in == v1, row, SUBLANES), axis=0, keepdims=True)
    e2 = jnp.where(row == i1, -jnp.inf, e_in)
    v2 = jnp.max(e2, axis=0, keepdims=True)
    i2 = jnp.min(jnp.where(e2 == v2, row, SUBLANES), axis=0, keepdims=True)
    t = jnp.exp(v2 - v1)
    w1 = p_group / (1.0 + t)
    w2 = p_group * t / (1.0 + t)
    id1 = g_idx * EXPERTS_PER_GROUP + i1
    id2 = g_idx * EXPERTS_PER_GROUP + i2
    wt_ref[...] = jnp.where(row == 0, w1, jnp.where(row == 1, w2, 0.0))

    erow = lax.broadcasted_iota(I32, (N_EXPERTS, tm), 0)
    oh1 = (erow == id1).astype(F32)
    oh2 = (erow == id2).astype(F32)
    c1 = _dot(oh1.astype(BF16), tri_ref[...])
    c2 = _dot(oh2.astype(BF16), tri_ref[...])
    tot1 = jnp.sum(oh1, axis=1, keepdims=True)
    tot2 = jnp.sum(oh2, axis=1, keepdims=True)
    base = run_ref[:, 0:1]
    r1 = jnp.sum(oh1 * (base + c1), axis=0, keepdims=True)
    r2 = jnp.sum(oh2 * (base + tot1 + c2), axis=0, keepdims=True)
    run_ref[...] = run_ref[...] + (tot1 + tot2)
    cnt_ref[...] = run_ref[...]
    eid_ref[...] = jnp.where(row == 0, id1, jnp.where(row == 1, id2, jnp.where(
        row == 2, r1.astype(I32), jnp.where(row == 3, r2.astype(I32), 0))))


def _merge(ya, ym, P, merge_b, x2, w_a, w_m, w_o, g2, w_r, b_r):
    T = x2.shape[0]
    tm = TOK_TILE
    const = lambda i: (0, 0)
    ti = jnp.arange(tm, dtype=I32)
    tri = (ti[:, None] < ti[None, :]).astype(BF16)
    return pl.pallas_call(
        _merge_kernel,
        grid=(T // tm,),
        in_specs=[
            pl.BlockSpec((tm, ATTN_Q_W), lambda i: (i, 0)),
            pl.BlockSpec((tm, MLSTM_W), lambda i: (i, 0)),
            pl.BlockSpec((tm, D_MODEL), lambda i: (i, 0)),
            pl.BlockSpec((tm, D_MODEL), lambda i: (i, 1)),
            pl.BlockSpec((1, 2 * D_MODEL), const),
            pl.BlockSpec((tm, D_MODEL), lambda i: (i, 0)),
            pl.BlockSpec((ATTN_Q_W, D_MODEL), const),
            pl.BlockSpec((MLSTM_W, D_MODEL), const),
            pl.BlockSpec((D_MODEL, D_MODEL), const),
            pl.BlockSpec((1, D_MODEL), const),
            pl.BlockSpec((LANES, D_MODEL), const),
            pl.BlockSpec((LANES, 1), const),
            pl.BlockSpec((tm, tm), const),
        ],
        out_specs=[
            pl.BlockSpec((tm, D_MODEL), lambda i: (i, 0)),
            pl.BlockSpec((tm * ROW_TILE, LANES), lambda i: (i, 0)),
            pl.BlockSpec((SUBLANES, tm), lambda i: (0, i)),
            pl.BlockSpec((SUBLANES, tm), lambda i: (0, i)),
            pl.BlockSpec((N_EXPERTS, LANES), const),
        ],
        out_shape=[
            jax.ShapeDtypeStruct((T, D_MODEL), F32),
            jax.ShapeDtypeStruct((T * ROW_TILE, LANES), F32),
            jax.ShapeDtypeStruct((SUBLANES, T), I32),
            jax.ShapeDtypeStruct((SUBLANES, T), F32),
            jax.ShapeDtypeStruct((N_EXPERTS, LANES), F32),
        ],
        scratch_shapes=[pltpu.VMEM((N_EXPERTS, LANES), F32)],
        compiler_params=pltpu.CompilerParams(
            dimension_semantics=("arbitrary",), vmem_limit_bytes=VMEM_LIMIT_BYTES),
        name="merge",
    )(ya, ym, P, P, merge_b, x2, w_a, w_m, w_o, g2, w_r, b_r, tri)


def _inv_kernel(dest_ref, pad_hbm, inv_hbm, inv_smem, sem, *, chunk):
    c = pl.program_id(0)

    @pl.when(c == 0)
    def _():
        cp = pltpu.make_async_copy(pad_hbm, inv_smem, sem)
        cp.start()
        cp.wait()

    def body(i, carry):
        inv_smem[dest_ref[0, 0, i]] = c * chunk + i
        return carry
    lax.fori_loop(0, chunk, body, 0, unroll=64)

    @pl.when(c == pl.num_programs(0) - 1)
    def _():
        cp = pltpu.make_async_copy(inv_smem, inv_hbm, sem)
        cp.start()
        cp.wait()


def _inverse_map(dest_flat, n_rows, pad_base):
    A = dest_flat.shape[0]
    chunk = min(A, 4096)
    nch = A // chunk
    pad_ids = pad_base + (jnp.arange(n_rows, dtype=I32) & (2 * MOE_BLK - 1))
    return pl.pallas_call(
        functools.partial(_inv_kernel, chunk=chunk),
        grid=(nch,),
        in_specs=[
            pl.BlockSpec((1, 1, chunk), lambda c: (c, 0, 0), memory_space=pltpu.SMEM),
            pl.BlockSpec(memory_space=pl.ANY),
        ],
        out_specs=pl.BlockSpec(memory_space=pl.ANY),
        scratch_shapes=[pltpu.SMEM((n_rows,), I32), pltpu.SemaphoreType.DMA],
        out_shape=jax.ShapeDtypeStruct((n_rows,), I32),
        compiler_params=pltpu.CompilerParams(dimension_semantics=("arbitrary",)),
        name="inverse_map",
    )(dest_flat.reshape(nch, 1, chunk), pad_ids)


def _moe_kernel(blk_e_ref, nused_ref,
                invp_ref, invc_ref, invn_ref, h2_hbm, wg_ref, wu_ref, wd_ref,
                y_hbm,
                xg_ref, ys_ref, xs_ref, wgb_ref, wub_ref, wdb_ref, gsem, ssem, *, n_tok):
    blk = MOE_BLK
    b = pl.program_id(0)
    nused = nused_ref[0]
    slot = lax.rem(b, 2)
    oslot = 1 - slot
    tok_mask = n_tok - 1
    n_lane_tiles = D_MODEL // LANES

    def gather_copy(r, a, slot_):
        return pltpu.make_async_copy(
            h2_hbm.at[pl.ds((a & tok_mask) * ROW_TILE, ROW_TILE), :],
            xg_ref.at[slot_, pl.ds(r * ROW_TILE, ROW_TILE), :],
            gsem.at[slot_])

    def scatter_copy(r, a, slot_):
        return pltpu.make_async_copy(
            ys_ref.at[slot_, pl.ds(r * ROW_TILE, ROW_TILE), :],
            y_hbm.at[pl.ds(a * ROW_TILE, ROW_TILE), :],
            ssem.at[slot_])

    def wait_rows(copy_fn, slot_):
        for _ in range(blk):
            copy_fn(0, 0, slot_).wait()

    @pl.when(b < nused)
    def _():
        @pl.when(b == 0)
        def _():
            ys_ref[...] = jnp.zeros(ys_ref.shape, F32)

            def prologue(r, carry):
                scatter_copy(r, TOP_K * n_tok + r, 0).start()
                gather_copy(r, invc_ref[0, 0, r], 0).start()
                return carry
            lax.fori_loop(0, blk, prologue, 0)

        changed = (b == 0) | (blk_e_ref[b] != blk_e_ref[jnp.maximum(b - 1, 0)])

        @pl.when(changed)
        def _():
            wgb_ref[...] = wg_ref[0].astype(BF16)
            wub_ref[...] = wu_ref[0].astype(BF16)
            wdb_ref[...] = wd_ref[0].astype(BF16)

        wait_rows(gather_copy, slot)
        per = blk // n_lane_tiles
        for j in range(n_lane_tiles):
            xs_ref[:, j * LANES:(j + 1) * LANES] = (
                xg_ref[slot, pl.ds(j, blk, stride=ROW_TILE), :].astype(BF16))
            for r in range(j * per, (j + 1) * per):
                gather_copy(r, invn_ref[0, 0, r], oslot).start(priority=r % 2)
        x = xs_ref[...]
        nw = 256
        acts = []
        r0 = 0
        n_up = D_EXPERT // nw
        n_dn = D_MODEL // nw
        per_up = blk // (2 * n_up)
        per_dn = (blk - n_up * per_up) // n_dn
        for n in range(n_up):
            g = _dot(x, wgb_ref[:, n * nw:(n + 1) * nw])
            u = _dot(x, wub_ref[:, n * nw:(n + 1) * nw])
            acts.append((g * jax.nn.sigmoid(g) * u).astype(BF16))
            for r in range(r0, r0 + per_up):
                scatter_copy(r, invp_ref[0, 0, r], oslot).start(priority=r % 2)
            r0 += per_up
        act = jnp.concatenate(acts, axis=1)
        wait_rows(scatter_copy, slot)
        for n in range(n_dn):
            y = _dot(act, wdb_ref[:, n * nw:(n + 1) * nw])
            for jj in range(nw // LANES):
                j = n * (nw // LANES) + jj
                ys_ref[slot, pl.ds(j, blk, stride=ROW_TILE), :] = y[:, jj * LANES:(jj + 1) * LANES]
            for r in range(r0, r0 + per_dn):
                scatter_copy(r, invp_ref[0, 0, r], oslot).start(priority=r % 2)
            r0 += per_dn
        assert r0 == blk

        @pl.when(b == nused - 1)
        def _():
            def epilogue(r, carry):
                scatter_copy(r, invc_ref[0, 0, r], slot).start()
                return carry
            lax.fori_loop(0, blk, epilogue, 0)
            wait_rows(scatter_copy, oslot)
            wait_rows(scatter_copy, slot)
            wait_rows(gather_copy, oslot)


def _moe(blk_e, nused, inv, h2g, w_gate, w_up, w_down, T):
    blk = MOE_BLK
    assert T & (T - 1) == 0, "token count must be a power of two for the id -> token mask"
    nblk = inv.shape[0] // blk
    inv3 = inv.reshape(nblk, 1, blk)
    virt = (TOP_K * T + blk + jnp.arange(blk, dtype=I32)).reshape(1, 1, blk)
    invp3 = jnp.concatenate([virt, inv3[:-1]], axis=0)
    grid_spec = pltpu.PrefetchScalarGridSpec(
        num_scalar_prefetch=2,
        grid=(nblk,),
        in_specs=[
            pl.BlockSpec((1, 1, blk), lambda b, e, n: (b, 0, 0), memory_space=pltpu.SMEM),
            pl.BlockSpec((1, 1, blk), lambda b, e, n: (b, 0, 0), memory_space=pltpu.SMEM),
            pl.BlockSpec((1, 1, blk), lambda b, e, n: (jnp.minimum(b + 1, nblk - 1), 0, 0),
                         memory_space=pltpu.SMEM),
            pl.BlockSpec(memory_space=pl.ANY),
            pl.BlockSpec((1, D_MODEL, D_EXPERT), lambda b, e, n: (e[b], 0, 0)),
            pl.BlockSpec((1, D_MODEL, D_EXPERT), lambda b, e, n: (e[b], 0, 0)),
            pl.BlockSpec((1, D_EXPERT, D_MODEL), lambda b, e, n: (e[b], 0, 0)),
        ],
        out_specs=pl.BlockSpec(memory_space=pl.ANY),
        scratch_shapes=[
            pltpu.VMEM((2, blk * ROW_TILE, LANES), F32),
            pltpu.VMEM((2, blk * ROW_TILE, LANES), F32),
            pltpu.VMEM((blk, D_MODEL), BF16),
            pltpu.VMEM((D_MODEL, D_EXPERT), BF16),
            pltpu.VMEM((D_MODEL, D_EXPERT), BF16),
            pltpu.VMEM((D_EXPERT, D_MODEL), BF16),
            pltpu.SemaphoreType.DMA((2,)),
            pltpu.SemaphoreType.DMA((2,)),
        ],
    )
    return pl.pallas_call(
        functools.partial(_moe_kernel, n_tok=T),
        grid_spec=grid_spec,
        out_shape=jax.ShapeDtypeStruct(((TOP_K * T + 2 * blk) * ROW_TILE, LANES), F32),
        compiler_params=pltpu.CompilerParams(
            dimension_semantics=("arbitrary",), vmem_limit_bytes=VMEM_LIMIT_BYTES),
        name="moe",
    )(blk_e, nused, invp3, inv3, inv3, h2g, w_gate, w_up, w_down)


def _combine_kernel(x1_ref, y0_ref, y1_ref, wt_ref, o_ref):
    tm = x1_ref.shape[0]
    r_i = lax.broadcasted_iota(I32, (LANES, LANES), 0)
    c_i = lax.broadcasted_iota(I32, (LANES, LANES), 1)
    eye = r_i == c_i
    for c in range(tm // LANES):
        rows = pl.ds(c * LANES, LANES)
        w0 = jnp.sum(jnp.where(eye, wt_ref[0:1, c * LANES:(c + 1) * LANES], 0.0), axis=1, keepdims=True)
        w1 = jnp.sum(jnp.where(eye, wt_ref[1:2, c * LANES:(c + 1) * LANES], 0.0), axis=1, keepdims=True)
        for j in range(D_MODEL // LANES):
            sl = slice(j * LANES, (j + 1) * LANES)
            tile = pl.ds(c * LANES * ROW_TILE + j, LANES, stride=ROW_TILE)
            o_ref[rows, sl] = x1_ref[rows, sl] + w0 * y0_ref[tile, :] + w1 * y1_ref[tile, :]


def _combine(x1, Y, wts):
    T = x1.shape[0]
    tm = TOK_TILE
    nt = T // tm
    return pl.pallas_call(
        _combine_kernel,
        grid=(nt,),
        in_specs=[
            pl.BlockSpec((tm, D_MODEL), lambda i: (i, 0)),
            pl.BlockSpec((tm * ROW_TILE, LANES), lambda i: (i, 0)),
            pl.BlockSpec((tm * ROW_TILE, LANES), lambda i: (i + nt, 0)),
            pl.BlockSpec((SUBLANES, tm), lambda i: (0, i)),
        ],
        out_specs=pl.BlockSpec((tm, D_MODEL), lambda i: (i, 0)),
        out_shape=jax.ShapeDtypeStruct((T, D_MODEL), F32),
        compiler_params=pltpu.CompilerParams(
            dimension_semantics=("arbitrary",), vmem_limit_bytes=VMEM_LIMIT_BYTES),
        name="combine",
    )(x1, Y, Y, wts)


def _t5_bucket(rel):
    half = N_REL_BUCKETS // 2
    max_exact = half // 2
    sign = jnp.where(rel > 0, half, 0)
    n = jnp.abs(rel)
    nf = jnp.maximum(n, 1).astype(jnp.float32)
    large = max_exact + (jnp.log(nf / max_exact) / math.log(REL_MAX_DISTANCE / max_exact)
                         * (half - max_exact)).astype(jnp.int32)
    large = jnp.minimum(large, half - 1)
    return sign + jnp.where(n < max_exact, n, large)


def _attn_bias(rel_table):
    qi = jnp.arange(ATTN_BLOCK)
    ks = jnp.arange(3 * ATTN_BLOCK) - ATTN_BLOCK
    rel = ks[None, :] - qi[:, None]
    onehot = jax.nn.one_hot(_t5_bucket(rel), N_REL_BUCKETS, dtype=F32)
    bias = jnp.einsum('qsb,bh->hqs', onehot, rel_table.astype(F32),
                      precision=lax.Precision.HIGHEST)
    bias = jnp.where((jnp.abs(rel) <= WINDOW)[None], bias, NEG_INF)
    not_prev = (ks >= 0)[None, None, :]
    not_next = (ks < ATTN_BLOCK)[None, None, :]
    return jnp.stack([bias,
                      jnp.where(not_prev, bias, NEG_INF),
                      jnp.where(not_next, bias, NEG_INF),
                      jnp.where(not_prev & not_next, bias, NEG_INF)], axis=0)


def _block_table(counts, T):
    blk = MOE_BLK
    nblk = (TOP_K * T) // blk + N_EXPERTS
    pcounts = ((counts + blk - 1) // blk) * blk
    pends = jnp.cumsum(pcounts)
    pstarts = (pends - pcounts).astype(I32)
    nused = (pends[-1] // blk).astype(I32)
    first_row = jnp.arange(nblk, dtype=I32) * blk
    blk_e = jnp.sum((first_row[:, None] >= pends[None, :]).astype(I32), axis=1)
    blk_e = jnp.minimum(blk_e, N_EXPERTS - 1)
    last_e = jnp.sum((((nused - 1) * blk) >= pends).astype(I32))
    blk_e = jnp.where(jnp.arange(nblk) < nused, blk_e, jnp.minimum(last_e, N_EXPERTS - 1)).astype(I32)
    return pstarts, blk_e, nused.reshape(1), nblk * blk


def kernel(x, norm1_g, w_in, attn_q_gain, attn_k_gain, attn_sink, rel_bias_table, mlstm_conv_w,
           mlstm_conv_b, mlstm_gate_b, mlstm_out_gain, w_branch_attn, w_branch_mlstm, merge_b, w_out,
           norm2_g, w_router_group, b_router_group, w_router_expert, b_router_expert,
           w_expert_gate, w_expert_up, w_expert_down):
    B, S, D = x.shape
    T = B * S
    depth = norm1_g.shape[0]
    x2 = x.reshape(T, D)
    bias = _attn_bias(rel_bias_table)
    for l in range(depth):
        w = w_in[l]
        w_main = jnp.concatenate([w[:, _OFF_MG:_W_IN], w[:, _OFF_QA:_OFF_VM], w[:, _OFF_OM:_OFF_GM]],
                                 axis=1).astype(BF16)
        w_vt = w[:, _OFF_VM:_OFF_OM].T.astype(BF16)
        w_gt = w[:, _OFF_GM:_OFF_MG].T.astype(BF16)
        gate_b = mlstm_gate_b[l].reshape(MLSTM_GATE_W, 1).astype(F32)
        P, QKO, VT, GT = _in_proj(x2, norm1_g[l][None, :], w_main, w_vt, w_gt, gate_b)

        qg = jnp.tile(attn_q_gain[l], LANES // HEAD_DIM_ATTN)[None, :]
        kg = jnp.tile(attn_k_gain[l], LANES // HEAD_DIM_ATTN)[None, :]
        ya = _attention(P, bias, attn_sink[l].astype(F32), qg, kg, B, S)

        ym = _mlstm(QKO, VT, GT, mlstm_conv_w[l], mlstm_conv_b[l][None, :], mlstm_out_gain[l], B, S)

        wr = jnp.zeros((LANES, D), F32)
        wr = wr.at[0:N_GROUPS].set(w_router_group[l].T)
        wr = wr.at[SUBLANES:SUBLANES + N_EXPERTS].set(w_router_expert[l].T)
        b_r = jnp.zeros((LANES,), F32)
        b_r = b_r.at[0:N_GROUPS].set(b_router_group[l])
        b_r = b_r.at[SUBLANES:SUBLANES + N_EXPERTS].set(b_router_expert[l])
        x1, h2g, route, wts, cnt = _merge(
            ya, ym, P, merge_b[l][None, :], x2, w_branch_attn[l].astype(BF16),
            w_branch_mlstm[l].astype(BF16), w_out[l].astype(BF16), norm2_g[l][None, :],
            wr.astype(BF16), b_r[:, None])

        pstarts, blk_e, nused, n_rows = _block_table(cnt[:, 0].astype(I32), T)
        eid_flat = route[0:TOP_K].reshape(TOP_K * T)
        seg_start = jnp.sum(jnp.where(eid_flat[None, :] == jnp.arange(N_EXPERTS, dtype=I32)[:, None],
                                      pstarts[:, None], 0), axis=0)
        dest = seg_start + route[TOP_K:2 * TOP_K].reshape(TOP_K * T)
        inv = _inverse_map(dest, n_rows, TOP_K * T)
        Y = _moe(blk_e, nused, inv, h2g, w_expert_gate[l], w_expert_up[l], w_expert_down[l], T)
        x2 = _combine(x1, Y, wts)
    return x2.reshape(B, S, D)
```

```python
import functools
import math

import jax
import jax.numpy as jnp
from jax import lax
from jax.experimental import pallas as pl
from jax.experimental.pallas import tpu as pltpu

F32 = jnp.float32
BF16 = jnp.bfloat16
I32 = jnp.int32

LANES = 128
SUBLANES = 8
VMEM_LIMIT_BYTES = 56 * 1024 * 1024

D_MODEL = 1024
N_HEADS_ATTN = 8
N_KV_HEADS = 2
HEAD_DIM_ATTN = 64
WINDOW = 128
ATTN_BLOCK = WINDOW
N_REL_BUCKETS = 32
REL_MAX_DISTANCE = 128
N_HEADS_MLSTM = 4
HEAD_DIM_MLSTM = 128
CONV_WIDTH = 5
ATTN_Q_W = N_HEADS_ATTN * HEAD_DIM_ATTN
ATTN_KV_W = N_KV_HEADS * HEAD_DIM_ATTN
MLSTM_W = N_HEADS_MLSTM * HEAD_DIM_MLSTM
MLSTM_GATE_W = 2 * 2 * N_HEADS_MLSTM
N_GROUPS = 4
EXPERTS_PER_GROUP = 8
N_EXPERTS = N_GROUPS * EXPERTS_PER_GROUP
TOP_K = 2
D_EXPERT = 512
RMS_EPS = 1e-6
NEG_INF = -1e30

_OFF_QA = 0
_OFF_VM = ATTN_Q_W + 2 * ATTN_KV_W + 2 * MLSTM_W
_OFF_OM = _OFF_VM + MLSTM_W
_OFF_GM = _OFF_OM + MLSTM_W
_OFF_MG = _OFF_GM + MLSTM_GATE_W
_W_IN = _OFF_MG + 2 * D_MODEL

P_MG = 0
P_QA = 2 * D_MODEL
P_KA = P_QA + ATTN_Q_W
P_VA = P_KA + ATTN_KV_W
P_QM = P_VA + ATTN_KV_W
P_KM = P_QM + MLSTM_W
P_OM = P_KM + MLSTM_W
P_W = P_OM + MLSTM_W

ATTN_ROWS = 32
MLSTM_L = 128
TOK_TILE = 512
MOE_BLK = 256
ROW_TILE = SUBLANES

_NT = (((1,), (1,)), ((), ()))


def _dot(a, b):
    return jnp.dot(a, b, preferred_element_type=F32)


def _dot_nt(a, b):
    return lax.dot_general(a, b, _NT, preferred_element_type=F32)


def _inproj_kernel(x_ref, g1_ref, wm_ref, wvt_ref, wgt_ref, gb_ref, p_ref, qko_ref, vt_ref, gt_ref):
    x = x_ref[...]
    h = x * lax.rsqrt(jnp.mean(x * x, axis=-1, keepdims=True) + RMS_EPS) * g1_ref[...]
    hb = h.astype(BF16)
    step = 512
    for c0 in range(0, P_QM, step):
        w = min(step, P_QM - c0)
        p_ref[:, c0:c0 + w] = _dot(hb, wm_ref[:, c0:c0 + w]).astype(BF16)
    for c0 in range(P_QM, P_W, step):
        res = _dot(hb, wm_ref[:, c0:c0 + step]).astype(BF16)
        for jj in range(step // LANES):
            qko_ref[(c0 - P_QM) // LANES + jj] = res[:, jj * LANES:(jj + 1) * LANES]
    vt = _dot_nt(wvt_ref[...], hb)
    gt = _dot_nt(wgt_ref[...], hb) + gb_ref[...]
    for cc in range(x.shape[0] // LANES):
        vt_ref[cc] = vt[:, cc * LANES:(cc + 1) * LANES].astype(BF16)
        gt_ref[cc] = gt[:, cc * LANES:(cc + 1) * LANES]


def _in_proj(x2, g1, w_main, w_vt, w_gt, gate_b):
    T = x2.shape[0]
    tm = TOK_TILE
    nck = tm // LANES
    return pl.pallas_call(
        _inproj_kernel,
        grid=(T // tm,),
        in_specs=[
            pl.BlockSpec((tm, D_MODEL), lambda i: (i, 0)),
            pl.BlockSpec((1, D_MODEL), lambda i: (0, 0)),
            pl.BlockSpec((D_MODEL, P_W), lambda i: (0, 0)),
            pl.BlockSpec((MLSTM_W, D_MODEL), lambda i: (0, 0)),
            pl.BlockSpec((MLSTM_GATE_W, D_MODEL), lambda i: (0, 0)),
            pl.BlockSpec((MLSTM_GATE_W, 1), lambda i: (0, 0)),
        ],
        out_specs=[
            pl.BlockSpec((tm, P_QM), lambda i: (i, 0)),
            pl.BlockSpec((3 * N_HEADS_MLSTM, tm, LANES), lambda i: (0, i, 0)),
            pl.BlockSpec((nck, MLSTM_W, LANES), lambda i: (i, 0, 0)),
            pl.BlockSpec((nck, MLSTM_GATE_W, LANES), lambda i: (i, 0, 0)),
        ],
        out_shape=[
            jax.ShapeDtypeStruct((T, P_QM), BF16),
            jax.ShapeDtypeStruct((3 * N_HEADS_MLSTM, T, LANES), BF16),
            jax.ShapeDtypeStruct((T // LANES, MLSTM_W, LANES), BF16),
            jax.ShapeDtypeStruct((T // LANES, MLSTM_GATE_W, LANES), F32),
        ],
        compiler_params=pltpu.CompilerParams(
            dimension_semantics=("arbitrary",), vmem_limit_bytes=VMEM_LIMIT_BYTES),
        name="in_proj",
    )(x2, g1, w_main, w_vt, w_gt, gate_b)


ATTN_PAIR = 2


def _attn_kernel(sink_ref, q_ref, kp_ref, kc_ref, kn_ref, vp_ref, vc_ref, vn_ref,
                 bias0_ref, bias1_ref, qg_ref, kg_ref, o_ref):
    half = HEAD_DIM_ATTN
    Q = ATTN_BLOCK
    G = ATTN_PAIR
    nk = (G + 2) * Q
    n_tiles = ATTN_Q_W // LANES
    group = N_HEADS_ATTN // N_KV_HEADS
    r_i = lax.broadcasted_iota(I32, (LANES, LANES), 0)
    c_i = lax.broadcasted_iota(I32, (LANES, LANES), 1)
    bd = ((r_i // half) == (c_i // half)).astype(BF16)
    lane = lax.broadcasted_iota(I32, (1, LANES), 1)
    lo_mask = lane < half

    parts = [kp_ref[...], kc_ref[...], kn_ref[...]] + [q_ref[:, j * LANES:(j + 1) * LANES] for j in range(n_tiles)]
    t = jnp.concatenate(parts, axis=0).astype(F32)
    tt = t * t
    hi = tt.astype(BF16)
    lo = (tt - hi.astype(F32)).astype(BF16)
    nrow = t.shape[0]
    ms2 = _dot(jnp.concatenate([hi, lo], axis=0), bd)
    ms = (ms2[0:nrow] + ms2[nrow:2 * nrow]) * (1.0 / half)
    tn = t * lax.rsqrt(ms + RMS_EPS)
    kn = tn[0:nk] * kg_ref[...]
    k_nat = kn.astype(BF16)
    k_rot = pltpu.roll(kn, half, axis=1).astype(BF16)
    qn = [tn[nk + j * G * Q:nk + (j + 1) * G * Q] * qg_ref[...] * (HEAD_DIM_ATTN ** -0.5)
          for j in range(n_tiles)]

    v = jnp.concatenate([vp_ref[...], vc_ref[...], vn_ref[...]], axis=0).astype(F32)
    v_rot = pltpu.roll(v, half, axis=1)
    zero = jnp.zeros_like(v)
    v_lo = [jnp.where(lo_mask, v, zero).astype(BF16), jnp.where(lo_mask, v_rot, zero).astype(BF16)]
    v_hi = [jnp.where(lo_mask, zero, v_rot).astype(BF16), jnp.where(lo_mask, zero, v).astype(BF16)]

    def q_masked(hd):
        keep = lo_mask if hd % 2 == 0 else ~lo_mask
        return jnp.where(keep, qn[hd // 2], 0.0).astype(BF16)

    heads_nat = [hd for hd in range(N_HEADS_ATTN) if (hd // group) == (hd % 2)]
    heads_rot = [hd for hd in range(N_HEADS_ATTN) if (hd // group) != (hd % 2)]
    scores = {}
    for kx, hds in ((k_nat, heads_nat), (k_rot, heads_rot)):
        s_all = _dot_nt(jnp.concatenate([q_masked(hd) for hd in hds], axis=0), kx)
        for i, hd in enumerate(hds):
            scores[hd] = s_all[i * G * Q:(i + 1) * G * Q]

    bias_refs = (bias0_ref, bias1_ref)
    for g in range(G):
        rows = slice(g * Q, (g + 1) * Q)
        win = slice(g * Q, (g + 3) * Q)
        for j in range(n_tiles):
            probs = []
            for hd in (2 * j, 2 * j + 1):
                s = scores[hd][rows, win] + bias_refs[g][0, hd]
                sink = sink_ref[hd]
                m = jnp.maximum(jnp.max(s, axis=1, keepdims=True), sink)
                e = jnp.exp(s - m)
                den = jnp.sum(e, axis=1, keepdims=True) + jnp.exp(sink - m)
                probs.append((e * (1.0 / den)).astype(BF16))
            kv = (2 * j) // group
            v_pair = jnp.concatenate([v_lo[kv][win], v_hi[kv][win]], axis=0)
            o_ref[rows, j * LANES:(j + 1) * LANES] = _dot(jnp.concatenate(probs, axis=1), v_pair).astype(BF16)


def _attention(P, bias, sink, qg, kg, B, S):
    Q = ATTN_BLOCK
    G = ATTN_PAIR
    nb = S // Q
    assert G == 2 and nb % G == 0
    npair = nb // G
    qcol = P_QA // ATTN_Q_W
    kcol = P_KA // ATTN_KV_W
    vcol = P_VA // ATTN_KV_W

    def edge(col, shift):
        def imap(b, n, sink_ref):
            nn = jnp.clip(n * G + shift, 0, nb - 1)
            return (b * nb + nn, col)
        return pl.BlockSpec((Q, ATTN_KV_W), imap)

    def own(col):
        return pl.BlockSpec((G * Q, ATTN_KV_W), lambda b, n, s: (b * npair + n, col))

    def bias_spec(imap):
        return pl.BlockSpec((1, N_HEADS_ATTN, Q, 3 * Q), imap)

    grid_spec = pltpu.PrefetchScalarGridSpec(
        num_scalar_prefetch=1,
        grid=(B, npair),
        in_specs=[
            pl.BlockSpec((G * Q, ATTN_Q_W), lambda b, n, s: (b * npair + n, qcol)),
            edge(kcol, -1), own(kcol), edge(kcol, G),
            edge(vcol, -1), own(vcol), edge(vcol, G),
            bias_spec(lambda b, n, s: ((n == 0).astype(I32), 0, 0, 0)),
            bias_spec(lambda b, n, s: (2 * (n == npair - 1).astype(I32), 0, 0, 0)),
            pl.BlockSpec((1, LANES), lambda b, n, s: (0, 0)),
            pl.BlockSpec((1, LANES), lambda b, n, s: (0, 0)),
        ],
        out_specs=pl.BlockSpec((G * Q, ATTN_Q_W), lambda b, n, s: (b * npair + n, 0)),
    )
    return pl.pallas_call(
        _attn_kernel,
        grid_spec=grid_spec,
        out_shape=jax.ShapeDtypeStruct((B * S, ATTN_Q_W), BF16),
        compiler_params=pltpu.CompilerParams(
            dimension_semantics=("arbitrary", "arbitrary"), vmem_limit_bytes=VMEM_LIMIT_BYTES),
        name="attn",
    )(sink, P, P, P, P, P, P, P, bias, bias, qg, kg)


def _log_sigmoid(x):
    return jnp.minimum(x, 0.0) - jnp.log1p(jnp.exp(-jnp.abs(x)))


def _mlstm_kernel(q_ref, k_ref, vt_ref, o_ref, gt_ref, cwq_ref, cwk_ref, cbq_ref, cbk_ref, og_ref,
                  ym_ref,
                  pad_ref, qs_ref, ks_ref, v1t_ref, ls_ref, cs_ref, ct_ref, m_ref, hs_ref):
    L = MLSTM_L
    S = q_ref.shape[0]
    nc = S // L
    hh = pl.program_id(1)
    H = N_HEADS_MLSTM

    halo = SUBLANES
    zpad = jnp.zeros((halo, LANES), F32)
    pad_ref[0:halo, :] = zpad
    pad_ref[halo + S:2 * halo + S, :] = zpad

    def conv_silu(src_ref, w_ref, b_ref, dst_ref, scale):
        pad_ref[halo:halo + S, :] = src_ref[...].astype(F32)
        for c in range(nc):
            acc = jnp.broadcast_to(b_ref[...], (L, LANES))
            for j in range(CONV_WIDTH):
                off = c * L + halo - CONV_WIDTH // 2 + j
                acc = acc + pad_ref[off:off + L, :] * w_ref[j:j + 1, :]
            y = acc * jax.nn.sigmoid(acc)
            dst_ref[c * L:(c + 1) * L, :] = (y * scale).astype(BF16)

    conv_silu(q_ref, cwq_ref, cbq_ref, qs_ref, 1.0)
    conv_silu(k_ref, cwk_ref, cbk_ref, ks_ref, HEAD_DIM_MLSTM ** -0.5)

    ones_blk = jnp.ones((L, LANES), BF16)
    for c in range(nc):
        v1t_ref[c, 0:L, :] = vt_ref[c]
        v1t_ref[c, L:2 * L, :] = ones_blk

    g_all = gt_ref[...].reshape(nc * MLSTM_GATE_W, LANES)
    ls = _log_sigmoid(g_all)
    lane = lax.broadcasted_iota(I32, (1, LANES), 1)
    pre = ls
    suf = ls
    d = 1
    while d < LANES:
        pre = pre + jnp.where(lane >= d, pltpu.roll(pre, d, axis=1), 0.0)
        suf = suf + jnp.where(lane < LANES - d, pltpu.roll(suf, LANES - d, axis=1), 0.0)
        d *= 2
    ls_ref[...] = ls.reshape(nc, MLSTM_GATE_W, LANES)
    cs_ref[0] = pre.reshape(nc, MLSTM_GATE_W, LANES)
    cs_ref[1] = suf.reshape(nc, MLSTM_GATE_W, LANES)

    ct_ref[...] = jnp.zeros(ct_ref.shape, F32)
    m_ref[...] = jnp.zeros(m_ref.shape, F32)

    r_i = lax.broadcasted_iota(I32, (L, L), 0)
    c_i = lax.broadcasted_iota(I32, (L, L), 1)
    masks = (c_i <= r_i, c_i >= r_i)

    def chunk_step(dr, c, accumulate):
        mask = masks[dr]
        row_i = dr * 2 * H + hh
        row_f = dr * 2 * H + H + hh
        ig_row = gt_ref[c, pl.ds(row_i, 1), :]
        lf_row = ls_ref[c, pl.ds(row_f, 1), :]
        a_row = ig_row - cs_ref[dr, c, pl.ds(row_f, 1), :]
        am = jnp.where(mask, a_row, -jnp.inf)
        cmax = jnp.max(am, axis=1, keepdims=True)
        bcol = jnp.sum(jnp.where(mask, lf_row, 0.0), axis=1, keepdims=True)
        m_row = m_ref[dr, 0:1, :]
        mm = jnp.maximum(cmax, m_row)
        w = jnp.exp(am - mm)
        inter = jnp.exp(m_row - mm)
        clamp = jnp.exp(-(bcol + mm))
        row0 = pl.multiple_of(c * L, L)
        q_c = qs_ref[pl.ds(row0, L), :]
        k_c = ks_ref[pl.ds(row0, L), :]
        sw = (_dot_nt(q_c, k_c) * w).astype(BF16)
        qi = (q_c.astype(F32) * inter).astype(BF16)
        v1t = v1t_ref[c]
        ct = ct_ref[dr]
        lhs = jnp.concatenate([sw, qi], axis=1)
        rhs_t = jnp.concatenate([v1t, ct.astype(BF16)], axis=1)
        nd = _dot_nt(lhs, rhs_t)
        hval = nd[:, 0:L] / jnp.maximum(jnp.abs(nd[:, L:2 * L]), clamp)
        if accumulate:
            hs_ref[pl.ds(row0, L), :] = hs_ref[pl.ds(row0, L), :] + hval
        else:
            hs_ref[pl.ds(row0, L), :] = hval
        m_last = jnp.maximum(jnp.max(a_row, axis=1, keepdims=True), m_row)
        decay = jnp.exp(m_row - m_last)
        ws_row = jnp.exp(a_row - m_last)
        ut = _dot((v1t.astype(F32) * ws_row).astype(BF16), k_c)
        ct_ref[dr] = decay * ct + ut
        m_new = jnp.sum(lf_row, axis=1, keepdims=True) + m_last
        m_ref[dr] = jnp.broadcast_to(m_new, (SUBLANES, LANES))

    def first_half(i, carry):
        chunk_step(0, i, False)
        chunk_step(1, nc - 1 - i, False)
        return carry

    def second_half(i, carry):
        chunk_step(0, i, True)
        chunk_step(1, nc - 1 - i, True)
        return carry

    lax.fori_loop(0, nc // 2, first_half, 0, unroll=True)
    lax.fori_loop(nc // 2, nc, second_half, 0, unroll=True)

    for c in range(nc):
        hsum = hs_ref[c * L:(c + 1) * L, :]
        y = hsum * lax.rsqrt(jnp.mean(hsum * hsum, axis=-1, keepdims=True) + RMS_EPS) * og_ref[0]
        og = jax.nn.sigmoid(o_ref[c * L:(c + 1) * L, :].astype(F32))
        ym_ref[c * L:(c + 1) * L, :] = (y * og).astype(BF16)


def _mlstm(QKO, VT, GT, conv_w, conv_b, out_gain, B, S):
    H = N_HEADS_MLSTM
    L = MLSTM_L
    nc = S // L
    return pl.pallas_call(
        _mlstm_kernel,
        grid=(B, H),
        in_specs=[
            pl.BlockSpec((None, S, LANES), lambda b, h: (h, b, 0)),
            pl.BlockSpec((None, S, LANES), lambda b, h: (H + h, b, 0)),
            pl.BlockSpec((nc, LANES, LANES), lambda b, h: (b, h, 0)),
            pl.BlockSpec((None, S, LANES), lambda b, h: (2 * H + h, b, 0)),
            pl.BlockSpec((nc, MLSTM_GATE_W, LANES), lambda b, h: (b, 0, 0)),
            pl.BlockSpec((CONV_WIDTH, LANES), lambda b, h: (0, h)),
            pl.BlockSpec((CONV_WIDTH, LANES), lambda b, h: (0, H + h)),
            pl.BlockSpec((1, LANES), lambda b, h: (0, h)),
            pl.BlockSpec((1, LANES), lambda b, h: (0, H + h)),
            pl.BlockSpec((1, 1, LANES), lambda b, h: (h, 0, 0)),
        ],
        out_specs=pl.BlockSpec((S, LANES), lambda b, h: (b, h)),
        out_shape=jax.ShapeDtypeStruct((B * S, MLSTM_W), BF16),
        scratch_shapes=[
            pltpu.VMEM((S + 2 * SUBLANES, LANES), F32),
            pltpu.VMEM((S, LANES), BF16),
            pltpu.VMEM((S, LANES), BF16),
            pltpu.VMEM((nc, 2 * L, LANES), BF16),
            pltpu.VMEM((nc, MLSTM_GATE_W, LANES), F32),
            pltpu.VMEM((2, nc, MLSTM_GATE_W, LANES), F32),
            pltpu.VMEM((2, 2 * L, LANES), F32),
            pltpu.VMEM((2, SUBLANES, LANES), F32),
            pltpu.VMEM((S, LANES), F32),
        ],
        compiler_params=pltpu.CompilerParams(
            dimension_semantics=("arbitrary", "arbitrary"), vmem_limit_bytes=VMEM_LIMIT_BYTES),
        name="mlstm",
    )(QKO, QKO, VT, QKO, GT, conv_w, conv_w, conv_b, conv_b, out_gain[:, None, :])


def _merge_kernel(ya_ref, ym_ref, mga_ref, mgm_ref, mb_ref, x_ref, wa_ref, wm_ref, wo_ref, g2_ref,
                  wr_ref, br_ref, tri_ref, x1_ref, h2_ref, eid_ref, wt_ref, cnt_ref, run_ref):
    tm = x_ref.shape[0]

    @pl.when(pl.program_id(0) == 0)
    def _():
        run_ref[...] = jnp.zeros(run_ref.shape, F32)
    a = _dot(ya_ref[...], wa_ref[...])
    mm = _dot(ym_ref[...], wm_ref[...])
    ga = jax.nn.sigmoid(mga_ref[...].astype(F32) + mb_ref[:, 0:D_MODEL])
    gm = jax.nn.sigmoid(mgm_ref[...].astype(F32) + mb_ref[:, D_MODEL:2 * D_MODEL])
    u = (ga * a + gm * mm).astype(BF16)
    x1 = x_ref[...] + _dot(u, wo_ref[...])
    x1_ref[...] = x1
    h2 = x1 * lax.rsqrt(jnp.mean(x1 * x1, axis=-1, keepdims=True) + RMS_EPS) * g2_ref[...]
    for j in range(D_MODEL // LANES):
        h2_ref[pl.ds(j, tm, stride=ROW_TILE), :] = h2[:, j * LANES:(j + 1) * LANES]

    lg = _dot_nt(wr_ref[...], h2.astype(BF16)) + br_ref[...]
    row = lax.broadcasted_iota(I32, (SUBLANES, tm), 0)
    gl = jnp.where(row < N_GROUPS, lg[0:SUBLANES], -jnp.inf)
    gmax = jnp.max(gl, axis=0, keepdims=True)
    g_idx = jnp.min(jnp.where(gl == gmax, row, SUBLANES), axis=0, keepdims=True)
    p_group = 1.0 / jnp.sum(jnp.exp(gl - gmax), axis=0, keepdims=True)
    e_in = lg[SUBLANES * N_GROUPS:SUBLANES * (N_GROUPS + 1)]
    for g in range(N_GROUPS - 2, -1, -1):
        e_in = jnp.where(g_idx == g, lg[SUBLANES * (g + 1):SUBLANES * (g + 2)], e_in)
    v1 = jnp.max(e_in, axis=0, keepdims=True)
    i1 = jnp.min(jnp.where(e_in == v1, row, SUBLANES), axis=0, keepdims=True)
    e2 = jnp.where(row == i1, -jnp.inf, e_in)
    v2 = jnp.max(e2, axis=0, keepdims=True)
    i2 = jnp.min(jnp.where(e2 == v2, row, SUBLANES), axis=0, keepdims=True)
    t = jnp.exp(v2 - v1)
    w1 = p_group / (1.0 + t)
    w2 = p_group * t / (1.0 + t)
    id1 = g_idx * EXPERTS_PER_GROUP + i1
    id2 = g_idx * EXPERTS_PER_GROUP + i2
    wt_ref[...] = jnp.where(row == 0, w1, jnp.where(row == 1, w2, 0.0))

    erow = lax.broadcasted_iota(I32, (N_EXPERTS, tm), 0)
    oh1 = (erow == id1).astype(F32)
    oh2 = (erow == id2).astype(F32)
    c1 = _dot(oh1.astype(BF16), tri_ref[...])
    c2 = _dot(oh2.astype(BF16), tri_ref[...])
    tot1 = jnp.sum(oh1, axis=1, keepdims=True)
    tot2 = jnp.sum(oh2, axis=1, keepdims=True)
    base = run_ref[:, 0:1]
    r1 = jnp.sum(oh1 * (base + c1), axis=0, keepdims=True)
    r2 = jnp.sum(oh2 * (base + tot1 + c2), axis=0, keepdims=True)
    run_ref[...] = run_ref[...] + (tot1 + tot2)
    cnt_ref[...] = run_ref[...]
    eid_ref[...] = jnp.where(row == 0, id1, jnp.where(row == 1, id2, jnp.where(
        row == 2, r1.astype(I32), jnp.where(row == 3, r2.astype(I32), 0))))


def _merge(ya, ym, P, merge_b, x2, w_a, w_m, w_o, g2, w_r, b_r):
    T = x2.shape[0]
    tm = TOK_TILE
    const = lambda i: (0, 0)
    ti = jnp.arange(tm, dtype=I32)
    tri = (ti[:, None] < ti[None, :]).astype(BF16)
    return pl.pallas_call(
        _merge_kernel,
        grid=(T // tm,),
        in_specs=[
            pl.BlockSpec((tm, ATTN_Q_W), lambda i: (i, 0)),
            pl.BlockSpec((tm, MLSTM_W), lambda i: (i, 0)),
            pl.BlockSpec((tm, D_MODEL), lambda i: (i, 0)),
            pl.BlockSpec((tm, D_MODEL), lambda i: (i, 1)),
            pl.BlockSpec((1, 2 * D_MODEL), const),
            pl.BlockSpec((tm, D_MODEL), lambda i: (i, 0)),
            pl.BlockSpec((ATTN_Q_W, D_MODEL), const),
            pl.BlockSpec((MLSTM_W, D_MODEL), const),
            pl.BlockSpec((D_MODEL, D_MODEL), const),
            pl.BlockSpec((1, D_MODEL), const),
            pl.BlockSpec((LANES, D_MODEL), const),
            pl.BlockSpec((LANES, 1), const),
            pl.BlockSpec((tm, tm), const),
        ],
        out_specs=[
            pl.BlockSpec((tm, D_MODEL), lambda i: (i, 0)),
            pl.BlockSpec((tm * ROW_TILE, LANES), lambda i: (i, 0)),
            pl.BlockSpec((SUBLANES, tm), lambda i: (0, i)),
            pl.BlockSpec((SUBLANES, tm), lambda i: (0, i)),
            pl.BlockSpec((N_EXPERTS, LANES), const),
        ],
        out_shape=[
            jax.ShapeDtypeStruct((T, D_MODEL), F32),
            jax.ShapeDtypeStruct((T * ROW_TILE, LANES), F32),
            jax.ShapeDtypeStruct((SUBLANES, T), I32),
            jax.ShapeDtypeStruct((SUBLANES, T), F32),
            jax.ShapeDtypeStruct((N_EXPERTS, LANES), F32),
        ],
        scratch_shapes=[pltpu.VMEM((N_EXPERTS, LANES), F32)],
        compiler_params=pltpu.CompilerParams(
            dimension_semantics=("arbitrary",), vmem_limit_bytes=VMEM_LIMIT_BYTES),
        name="merge",
    )(ya, ym, P, P, merge_b, x2, w_a, w_m, w_o, g2, w_r, b_r, tri)


def _inv_kernel(dest_ref, pad_hbm, inv_hbm, inv_smem, sem, *, chunk):
    c = pl.program_id(0)

    @pl.when(c == 0)
    def _():
        cp = pltpu.make_async_copy(pad_hbm, inv_smem, sem)
        cp.start()
        cp.wait()

    def body(i, carry):
        inv_smem[dest_ref[0, 0, i]] = c * chunk + i
        return carry
    lax.fori_loop(0, chunk, body, 0, unroll=64)

    @pl.when(c == pl.num_programs(0) - 1)
    def _():
        cp = pltpu.make_async_copy(inv_smem, inv_hbm, sem)
        cp.start()
        cp.wait()


def _inverse_map(dest_flat, n_rows, pad_base):
    A = dest_flat.shape[0]
    chunk = min(A, 4096)
    nch = A // chunk
    pad_ids = pad_base + (jnp.arange(n_rows, dtype=I32) & (2 * MOE_BLK - 1))
    return pl.pallas_call(
        functools.partial(_inv_kernel, chunk=chunk),
        grid=(nch,),
        in_specs=[
            pl.BlockSpec((1, 1, chunk), lambda c: (c, 0, 0), memory_space=pltpu.SMEM),
            pl.BlockSpec(memory_space=pl.ANY),
        ],
        out_specs=pl.BlockSpec(memory_space=pl.ANY),
        scratch_shapes=[pltpu.SMEM((n_rows,), I32), pltpu.SemaphoreType.DMA],
        out_shape=jax.ShapeDtypeStruct((n_rows,), I32),
        compiler_params=pltpu.CompilerParams(dimension_semantics=("arbitrary",)),
        name="inverse_map",
    )(dest_flat.reshape(nch, 1, chunk), pad_ids)


def _moe_kernel(blk_e_ref, nused_ref,
                invp_ref, invc_ref, invn_ref, h2_hbm, wg_ref, wu_ref, wd_ref,
                y_hbm,
                xg_ref, ys_ref, xs_ref, wgb_ref, wub_ref, wdb_ref, gsem, ssem, *, n_tok):
    blk = MOE_BLK
    b = pl.program_id(0)
    nused = nused_ref[0]
    slot = lax.rem(b, 2)
    oslot = 1 - slot
    tok_mask = n_tok - 1
    n_lane_tiles = D_MODEL // LANES

    def gather_copy(r, a, slot_):
        return pltpu.make_async_copy(
            h2_hbm.at[pl.ds((a & tok_mask) * ROW_TILE, ROW_TILE), :],
            xg_ref.at[slot_, pl.ds(r * ROW_TILE, ROW_TILE), :],
            gsem.at[slot_])

    def scatter_copy(r, a, slot_):
        return pltpu.make_async_copy(
            ys_ref.at[slot_, pl.ds(r * ROW_TILE, ROW_TILE), :],
            y_hbm.at[pl.ds(a * ROW_TILE, ROW_TILE), :],
            ssem.at[slot_])

    def wait_rows(copy_fn, slot_):
        for _ in range(blk):
            copy_fn(0, 0, slot_).wait()

    @pl.when(b < nused)
    def _():
        @pl.when(b == 0)
        def _():
            ys_ref[...] = jnp.zeros(ys_ref.shape, F32)

            def prologue(r, carry):
                scatter_copy(r, TOP_K * n_tok + r, 0).start()
                gather_copy(r, invc_ref[0, 0, r], 0).start()
                return carry
            lax.fori_loop(0, blk, prologue, 0)

        changed = (b == 0) | (blk_e_ref[b] != blk_e_ref[jnp.maximum(b - 1, 0)])

        @pl.when(changed)
        def _():
            wgb_ref[...] = wg_ref[0].astype(BF16)
            wub_ref[...] = wu_ref[0].astype(BF16)
            wdb_ref[...] = wd_ref[0].astype(BF16)

        wait_rows(gather_copy, slot)
        per = blk // n_lane_tiles
        for j in range(n_lane_tiles):
            xs_ref[:, j * LANES:(j + 1) * LANES] = (
                xg_ref[slot, pl.ds(j, blk, stride=ROW_TILE), :].astype(BF16))
            for r in range(j * per, (j + 1) * per):
                gather_copy(r, invn_ref[0, 0, r], oslot).start(priority=r % 2)
        x = xs_ref[...]
        nw = 256
        acts = []
        r0 = 0
        n_up = D_EXPERT // nw
        n_dn = D_MODEL // nw
        per_up = blk // (2 * n_up)
        per_dn = (blk - n_up * per_up) // n_dn
        for n in range(n_up):
            g = _dot(x, wgb_ref[:, n * nw:(n + 1) * nw])
            u = _dot(x, wub_ref[:, n * nw:(n + 1) * nw])
            acts.append((g * jax.nn.sigmoid(g) * u).astype(BF16))
            for r in range(r0, r0 + per_up):
                scatter_copy(r, invp_ref[0, 0, r], oslot).start(priority=r % 2)
            r0 += per_up
        act = jnp.concatenate(acts, axis=1)
        wait_rows(scatter_copy, slot)
        for n in range(n_dn):
            y = _dot(act, wdb_ref[:, n * nw:(n + 1) * nw])
            for jj in range(nw // LANES):
                j = n * (nw // LANES) + jj
                ys_ref[slot, pl.ds(j, blk, stride=ROW_TILE), :] = y[:, jj * LANES:(jj + 1) * LANES]
            for r in range(r0, r0 + per_dn):
                scatter_copy(r, invp_ref[0, 0, r], oslot).start(priority=r % 2)
            r0 += per_dn
        assert r0 == blk

        @pl.when(b == nused - 1)
        def _():
            def epilogue(r, carry):
                scatter_copy(r, invc_ref[0, 0, r], slot).start()
                return carry
            lax.fori_loop(0, blk, epilogue, 0)
            wait_rows(scatter_copy, oslot)
            wait_rows(scatter_copy, slot)
            wait_rows(gather_copy, oslot)


def _moe(blk_e, nused, inv, h2g, w_gate, w_up, w_down, T):
    blk = MOE_BLK
    assert T & (T - 1) == 0, "token count must be a power of two for the id -> token mask"
    nblk = inv.shape[0] // blk
    inv3 = inv.reshape(nblk, 1, blk)
    virt = (TOP_K * T + blk + jnp.arange(blk, dtype=I32)).reshape(1, 1, blk)
    invp3 = jnp.concatenate([virt, inv3[:-1]], axis=0)
    grid_spec = pltpu.PrefetchScalarGridSpec(
        num_scalar_prefetch=2,
        grid=(nblk,),
        in_specs=[
            pl.BlockSpec((1, 1, blk), lambda b, e, n: (b, 0, 0), memory_space=pltpu.SMEM),
            pl.BlockSpec((1, 1, blk), lambda b, e, n: (b, 0, 0), memory_space=pltpu.SMEM),
            pl.BlockSpec((1, 1, blk), lambda b, e, n: (jnp.minimum(b + 1, nblk - 1), 0, 0),
                         memory_space=pltpu.SMEM),
            pl.BlockSpec(memory_space=pl.ANY),
            pl.BlockSpec((1, D_MODEL, D_EXPERT), lambda b, e, n: (e[b], 0, 0)),
            pl.BlockSpec((1, D_MODEL, D_EXPERT), lambda b, e, n: (e[b], 0, 0)),
            pl.BlockSpec((1, D_EXPERT, D_MODEL), lambda b, e, n: (e[b], 0, 0)),
        ],
        out_specs=pl.BlockSpec(memory_space=pl.ANY),
        scratch_shapes=[
            pltpu.VMEM((2, blk * ROW_TILE, LANES), F32),
            pltpu.VMEM((2, blk * ROW_TILE, LANES), F32),
            pltpu.VMEM((blk, D_MODEL), BF16),
            pltpu.VMEM((D_MODEL, D_EXPERT), BF16),
            pltpu.VMEM((D_MODEL, D_EXPERT), BF16),
            pltpu.VMEM((D_EXPERT, D_MODEL), BF16),
            pltpu.SemaphoreType.DMA((2,)),
            pltpu.SemaphoreType.DMA((2,)),
        ],
    )
    return pl.pallas_call(
        functools.partial(_moe_kernel, n_tok=T),
        grid_spec=grid_spec,
        out_shape=jax.ShapeDtypeStruct(((TOP_K * T + 2 * blk) * ROW_TILE, LANES), F32),
        compiler_params=pltpu.CompilerParams(
            dimension_semantics=("arbitrary",), vmem_limit_bytes=VMEM_LIMIT_BYTES),
        name="moe",
    )(blk_e, nused, invp3, inv3, inv3, h2g, w_gate, w_up, w_down)


def _combine_kernel(x1_ref, y0_ref, y1_ref, wt_ref, o_ref):
    tm = x1_ref.shape[0]
    r_i = lax.broadcasted_iota(I32, (LANES, LANES), 0)
    c_i = lax.broadcasted_iota(I32, (LANES, LANES), 1)
    eye = r_i == c_i
    for c in range(tm // LANES):
        rows = pl.ds(c * LANES, LANES)
        w0 = jnp.sum(jnp.where(eye, wt_ref[0:1, c * LANES:(c + 1) * LANES], 0.0), axis=1, keepdims=True)
        w1 = jnp.sum(jnp.where(eye, wt_ref[1:2, c * LANES:(c + 1) * LANES], 0.0), axis=1, keepdims=True)
        for j in range(D_MODEL // LANES):
            sl = slice(j * LANES, (j + 1) * LANES)
            tile = pl.ds(c * LANES * ROW_TILE + j, LANES, stride=ROW_TILE)
            o_ref[rows, sl] = x1_ref[rows, sl] + w0 * y0_ref[tile, :] + w1 * y1_ref[tile, :]


def _combine(x1, Y, wts):
    T = x1.shape[0]
    tm = TOK_TILE
    nt = T // tm
    return pl.pallas_call(
        _combine_kernel,
        grid=(nt,),
        in_specs=[
            pl.BlockSpec((tm, D_MODEL), lambda i: (i, 0)),
            pl.BlockSpec((tm * ROW_TILE, LANES), lambda i: (i, 0)),
            pl.BlockSpec((tm * ROW_TILE, LANES), lambda i: (i + nt, 0)),
            pl.BlockSpec((SUBLANES, tm), lambda i: (0, i)),
        ],
        out_specs=pl.BlockSpec((tm, D_MODEL), lambda i: (i, 0)),
        out_shape=jax.ShapeDtypeStruct((T, D_MODEL), F32),
        compiler_params=pltpu.CompilerParams(
            dimension_semantics=("arbitrary",), vmem_limit_bytes=VMEM_LIMIT_BYTES),
        name="combine",
    )(x1, Y, Y, wts)


def _t5_bucket(rel):
    half = N_REL_BUCKETS // 2
    max_exact = half // 2
    sign = jnp.where(rel > 0, half, 0)
    n = jnp.abs(rel)
    nf = jnp.maximum(n, 1).astype(jnp.float32)
    large = max_exact + (jnp.log(nf / max_exact) / math.log(REL_MAX_DISTANCE / max_exact)
                         * (half - max_exact)).astype(jnp.int32)
    large = jnp.minimum(large, half - 1)
    return sign + jnp.where(n < max_exact, n, large)


def _attn_bias(rel_table):
    qi = jnp.arange(ATTN_BLOCK)
    ks = jnp.arange(3 * ATTN_BLOCK) - ATTN_BLOCK
    rel = ks[None, :] - qi[:, None]
    onehot = jax.nn.one_hot(_t5_bucket(rel), N_REL_BUCKETS, dtype=F32)
    bias = jnp.einsum('qsb,bh->hqs', onehot, rel_table.astype(F32),
                      precision=lax.Precision.HIGHEST)
    bias = jnp.where((jnp.abs(rel) <= WINDOW)[None], bias, NEG_INF)
    not_prev = (ks >= 0)[None, None, :]
    not_next = (ks < ATTN_BLOCK)[None, None, :]
    return jnp.stack([bias,
                      jnp.where(not_prev, bias, NEG_INF),
                      jnp.where(not_next, bias, NEG_INF),
                      jnp.where(not_prev & not_next, bias, NEG_INF)], axis=0)


def _block_table(counts, T):
    blk = MOE_BLK
    nblk = (TOP_K * T) // blk + N_EXPERTS
    pcounts = ((counts + blk - 1) // blk) * blk
    pends = jnp.cumsum(pcounts)
    pstarts = (pends - pcounts).astype(I32)
    nused = (pends[-1] // blk).astype(I32)
    first_row = jnp.arange(nblk, dtype=I32) * blk
    blk_e = jnp.sum((first_row[:, None] >= pends[None, :]).astype(I32), axis=1)
    blk_e = jnp.minimum(blk_e, N_EXPERTS - 1)
    last_e = jnp.sum((((nused - 1) * blk) >= pends).astype(I32))
    blk_e = jnp.where(jnp.arange(nblk) < nused, blk_e, jnp.minimum(last_e, N_EXPERTS - 1)).astype(I32)
    return pstarts, blk_e, nused.reshape(1), nblk * blk


def kernel(x, norm1_g, w_in, attn_q_gain, attn_k_gain, attn_sink, rel_bias_table, mlstm_conv_w,
           mlstm_conv_b, mlstm_gate_b, mlstm_out_gain, w_branch_attn, w_branch_mlstm, merge_b, w_out,
           norm2_g, w_router_group, b_router_group, w_router_expert, b_router_expert,
           w_expert_gate, w_expert_up, w_expert_down):
    B, S, D = x.shape
    T = B * S
    depth = norm1_g.shape[0]
    x2 = x.reshape(T, D)
    bias = _attn_bias(rel_bias_table)
    for l in range(depth):
        w = w_in[l]
        w_main = jnp.concatenate([w[:, _OFF_MG:_W_IN], w[:, _OFF_QA:_OFF_VM], w[:, _OFF_OM:_OFF_GM]],
                                 axis=1).astype(BF16)
        w_vt = w[:, _OFF_VM:_OFF_OM].T.astype(BF16)
        w_gt = w[:, _OFF_GM:_OFF_MG].T.astype(BF16)
        gate_b = mlstm_gate_b[l].reshape(MLSTM_GATE_W, 1).astype(F32)
        P, QKO, VT, GT = _in_proj(x2, norm1_g[l][None, :], w_main, w_vt, w_gt, gate_b)

        qg = jnp.tile(attn_q_gain[l], LANES // HEAD_DIM_ATTN)[None, :]
        kg = jnp.tile(attn_k_gain[l], LANES // HEAD_DIM_ATTN)[None, :]
        ya = _attention(P, bias, attn_sink[l].astype(F32), qg, kg, B, S)

        ym = _mlstm(QKO, VT, GT, mlstm_conv_w[l], mlstm_conv_b[l][None, :], mlstm_out_gain[l], B, S)

        wr = jnp.zeros((LANES, D), F32)
        wr = wr.at[0:N_GROUPS].set(w_router_group[l].T)
        wr = wr.at[SUBLANES:SUBLANES + N_EXPERTS].set(w_router_expert[l].T)
        b_r = jnp.zeros((LANES,), F32)
        b_r = b_r.at[0:N_GROUPS].set(b_router_group[l])
        b_r = b_r.at[SUBLANES:SUBLANES + N_EXPERTS].set(b_router_expert[l])
        x1, h2g, route, wts, cnt = _merge(
            ya, ym, P, merge_b[l][None, :], x2, w_branch_attn[l].astype(BF16),
            w_branch_mlstm[l].astype(BF16), w_out[l].astype(BF16), norm2_g[l][None, :],
            wr.astype(BF16), b_r[:, None])

        pstarts, blk_e, nused, n_rows = _block_table(cnt[:, 0].astype(I32), T)
        eid_flat = route[0:TOP_K].reshape(TOP_K * T)
        seg_start = jnp.sum(jnp.where(eid_flat[None, :] == jnp.arange(N_EXPERTS, dtype=I32)[:, None],
                                      pstarts[:, None], 0), axis=0)
        dest = seg_start + route[TOP_K:2 * TOP_K].reshape(TOP_K * T)
        inv = _inverse_map(dest, n_rows, TOP_K * T)
        Y = _moe(blk_e, nused, inv, h2g, w_expert_gate[l], w_expert_up[l], w_expert_down[l], T)
        x2 = _combine(x1, Y, wts)
    return x2.reshape(B, S, D)
```

```python
import functools
import math

import jax
import jax.numpy as jnp
from jax import lax
from jax.experimental import pallas as pl
from jax.experimental.pallas import tpu as pltpu

F32 = jnp.float32
BF16 = jnp.bfloat16
I32 = jnp.int32

LANES = 128
SUBLANES = 8
VMEM_LIMIT_BYTES = 56 * 1024 * 1024

D_MODEL = 1024
N_HEADS_ATTN = 8
N_KV_HEADS = 2
HEAD_DIM_ATTN = 64
WINDOW = 128
ATTN_BLOCK = WINDOW
N_REL_BUCKETS = 32
REL_MAX_DISTANCE = 128
N_HEADS_MLSTM = 4
HEAD_DIM_MLSTM = 128
CONV_WIDTH = 5
ATTN_Q_W = N_HEADS_ATTN * HEAD_DIM_ATTN
ATTN_KV_W = N_KV_HEADS * HEAD_DIM_ATTN
MLSTM_W = N_HEADS_MLSTM * HEAD_DIM_MLSTM
MLSTM_GATE_W = 2 * 2 * N_HEADS_MLSTM
N_GROUPS = 4
EXPERTS_PER_GROUP = 8
N_EXPERTS = N_GROUPS * EXPERTS_PER_GROUP
TOP_K = 2
D_EXPERT = 512
RMS_EPS = 1e-6
NEG_INF = -1e30

_OFF_QA = 0
_OFF_VM = ATTN_Q_W + 2 * ATTN_KV_W + 2 * MLSTM_W
_OFF_OM = _OFF_VM + MLSTM_W
_OFF_GM = _OFF_OM + MLSTM_W
_OFF_MG = _OFF_GM + MLSTM_GATE_W
_W_IN = _OFF_MG + 2 * D_MODEL

P_MG = 0
P_QA = 2 * D_MODEL
P_KA = P_QA + ATTN_Q_W
P_VA = P_KA + ATTN_KV_W
P_QM = P_VA + ATTN_KV_W
P_KM = P_QM + MLSTM_W
P_OM = P_KM + MLSTM_W
P_W = P_OM + MLSTM_W

ATTN_ROWS = 32
MLSTM_L = 128
TOK_TILE = 512
MOE_BLK = 256
ROW_TILE = SUBLANES

_NT = (((1,), (1,)), ((), ()))


def _dot(a, b):
    return jnp.dot(a, b, preferred_element_type=F32)


def _dot_nt(a, b):
    return lax.dot_general(a, b, _NT, preferred_element_type=F32)


def _inproj_kernel(x_ref, g1_ref, wm_ref, wvt_ref, wgt_ref, gb_ref, p_ref, qko_ref, vt_ref, gt_ref):
    x = x_ref[...]
    h = x * lax.rsqrt(jnp.mean(x * x, axis=-1, keepdims=True) + RMS_EPS) * g1_ref[...]
    hb = h.astype(BF16)
    step = 512
    for c0 in range(0, P_QM, step):
        w = min(step, P_QM - c0)
        p_ref[:, c0:c0 + w] = _dot(hb, wm_ref[:, c0:c0 + w]).astype(BF16)
    for c0 in range(P_QM, P_W, step):
        res = _dot(hb, wm_ref[:, c0:c0 + step]).astype(BF16)
        for jj in range(step // LANES):
            qko_ref[(c0 - P_QM) // LANES + jj] = res[:, jj * LANES:(jj + 1) * LANES]
    vt = _dot_nt(wvt_ref[...], hb)
    gt = _dot_nt(wgt_ref[...], hb) + gb_ref[...]
    for cc in range(x.shape[0] // LANES):
        vt_ref[cc] = vt[:, cc * LANES:(cc + 1) * LANES].astype(BF16)
        gt_ref[cc] = gt[:, cc * LANES:(cc + 1) * LANES]


def _in_proj(x2, g1, w_main, w_vt, w_gt, gate_b):
    T = x2.shape[0]
    tm = TOK_TILE
    nck = tm // LANES
    return pl.pallas_call(
        _inproj_kernel,
        grid=(T // tm,),
        in_specs=[
            pl.BlockSpec((tm, D_MODEL), lambda i: (i, 0)),
            pl.BlockSpec((1, D_MODEL), lambda i: (0, 0)),
            pl.BlockSpec((D_MODEL, P_W), lambda i: (0, 0)),
            pl.BlockSpec((MLSTM_W, D_MODEL), lambda i: (0, 0)),
            pl.BlockSpec((MLSTM_GATE_W, D_MODEL), lambda i: (0, 0)),
            pl.BlockSpec((MLSTM_GATE_W, 1), lambda i: (0, 0)),
        ],
        out_specs=[
            pl.BlockSpec((tm, P_QM), lambda i: (i, 0)),
            pl.BlockSpec((3 * N_HEADS_MLSTM, tm, LANES), lambda i: (0, i, 0)),
            pl.BlockSpec((nck, MLSTM_W, LANES), lambda i: (i, 0, 0)),
            pl.BlockSpec((nck, MLSTM_GATE_W, LANES), lambda i: (i, 0, 0)),
        ],
        out_shape=[
            jax.ShapeDtypeStruct((T, P_QM), BF16),
            jax.ShapeDtypeStruct((3 * N_HEADS_MLSTM, T, LANES), BF16),
            jax.ShapeDtypeStruct((T // LANES, MLSTM_W, LANES), BF16),
            jax.ShapeDtypeStruct((T // LANES, MLSTM_GATE_W, LANES), F32),
        ],
        compiler_params=pltpu.CompilerParams(
            dimension_semantics=("arbitrary",), vmem_limit_bytes=VMEM_LIMIT_BYTES),
        name="in_proj",
    )(x2, g1, w_main, w_vt, w_gt, gate_b)


ATTN_PAIR = 4


def _attn_kernel(sink_ref, q_ref, kp_ref, kc_ref, kn_ref, vp_ref, vc_ref, vn_ref,
                 bias_first_ref, bias_mid_ref, bias_last_ref, qg_ref, kg_ref, o_ref):
    half = HEAD_DIM_ATTN
    Q = ATTN_BLOCK
    G = ATTN_PAIR
    nk = (G + 2) * Q
    n_tiles = ATTN_Q_W // LANES
    group = N_HEADS_ATTN // N_KV_HEADS
    r_i = lax.broadcasted_iota(I32, (LANES, LANES), 0)
    c_i = lax.broadcasted_iota(I32, (LANES, LANES), 1)
    bd = ((r_i // half) == (c_i // half)).astype(BF16)
    lane = lax.broadcasted_iota(I32, (1, LANES), 1)
    lo_mask = lane < half

    parts = [kp_ref[...], kc_ref[...], kn_ref[...]] + [q_ref[:, j * LANES:(j + 1) * LANES] for j in range(n_tiles)]
    t = jnp.concatenate(parts, axis=0).astype(F32)
    tt = t * t
    hi = tt.astype(BF16)
    lo = (tt - hi.astype(F32)).astype(BF16)
    nrow = t.shape[0]
    ms2 = _dot(jnp.concatenate([hi, lo], axis=0), bd)
    ms = (ms2[0:nrow] + ms2[nrow:2 * nrow]) * (1.0 / half)
    tn = t * lax.rsqrt(ms + RMS_EPS)
    kn = tn[0:nk] * kg_ref[...]
    k_nat = kn.astype(BF16)
    k_rot = pltpu.roll(kn, half, axis=1).astype(BF16)
    qn = [tn[nk + j * G * Q:nk + (j + 1) * G * Q] * qg_ref[...] * (HEAD_DIM_ATTN ** -0.5)
          for j in range(n_tiles)]

    v = jnp.concatenate([vp_ref[...], vc_ref[...], vn_ref[...]], axis=0).astype(F32)
    v_rot = pltpu.roll(v, half, axis=1)
    zero = jnp.zeros_like(v)
    v_lo = [jnp.where(lo_mask, v, zero).astype(BF16), jnp.where(lo_mask, v_rot, zero).astype(BF16)]
    v_hi = [jnp.where(lo_mask, zero, v_rot).astype(BF16), jnp.where(lo_mask, zero, v).astype(BF16)]

    def q_masked(hd):
        keep = lo_mask if hd % 2 == 0 else ~lo_mask
        return jnp.where(keep, qn[hd // 2], 0.0).astype(BF16)

    heads_nat = [hd for hd in range(N_HEADS_ATTN) if (hd // group) == (hd % 2)]
    heads_rot = [hd for hd in range(N_HEADS_ATTN) if (hd // group) != (hd % 2)]
    scores = {}
    for kx, hds in ((k_nat, heads_nat), (k_rot, heads_rot)):
        s_all = _dot_nt(jnp.concatenate([q_masked(hd) for hd in hds], axis=0), kx)
        for i, hd in enumerate(hds):
            scores[hd] = s_all[i * G * Q:(i + 1) * G * Q]

    bias_refs = (bias_first_ref,) + (bias_mid_ref,) * (G - 2) + (bias_last_ref,)
    for g in range(G):
        rows = slice(g * Q, (g + 1) * Q)
        win = slice(g * Q, (g + 3) * Q)
        for j in range(n_tiles):
            probs = []
            for hd in (2 * j, 2 * j + 1):
                s = scores[hd][rows, win] + bias_refs[g][0, hd]
                sink = sink_ref[hd]
                m = jnp.maximum(jnp.max(s, axis=1, keepdims=True), sink)
                e = jnp.exp(s - m)
                den = jnp.sum(e, axis=1, keepdims=True) + jnp.exp(sink - m)
                probs.append((e * (1.0 / den)).astype(BF16))
            kv = (2 * j) // group
            v_pair = jnp.concatenate([v_lo[kv][win], v_hi[kv][win]], axis=0)
            o_ref[rows, j * LANES:(j + 1) * LANES] = _dot(jnp.concatenate(probs, axis=1), v_pair).astype(BF16)


def _attention(P, bias, sink, qg, kg, B, S):
    Q = ATTN_BLOCK
    G = ATTN_PAIR
    nb = S // Q
    assert G >= 2 and nb % G == 0
    npair = nb // G
    qcol = P_QA // ATTN_Q_W
    kcol = P_KA // ATTN_KV_W
    vcol = P_VA // ATTN_KV_W

    def edge(col, shift):
        def imap(b, n, sink_ref):
            nn = jnp.clip(n * G + shift, 0, nb - 1)
            return (b * nb + nn, col)
        return pl.BlockSpec((Q, ATTN_KV_W), imap)

    def own(col):
        return pl.BlockSpec((G * Q, ATTN_KV_W), lambda b, n, s: (b * npair + n, col))

    def bias_spec(imap):
        return pl.BlockSpec((1, N_HEADS_ATTN, Q, 3 * Q), imap)

    grid_spec = pltpu.PrefetchScalarGridSpec(
        num_scalar_prefetch=1,
        grid=(B, npair),
        in_specs=[
            pl.BlockSpec((G * Q, ATTN_Q_W), lambda b, n, s: (b * npair + n, qcol)),
            edge(kcol, -1), own(kcol), edge(kcol, G),
            edge(vcol, -1), own(vcol), edge(vcol, G),
            bias_spec(lambda b, n, s: ((n == 0).astype(I32), 0, 0, 0)),
            bias_spec(lambda b, n, s: (0, 0, 0, 0)),
            bias_spec(lambda b, n, s: (2 * (n == npair - 1).astype(I32), 0, 0, 0)),
            pl.BlockSpec((1, LANES), lambda b, n, s: (0, 0)),
            pl.BlockSpec((1, LANES), lambda b, n, s: (0, 0)),
        ],
        out_specs=pl.BlockSpec((G * Q, ATTN_Q_W), lambda b, n, s: (b * npair + n, 0)),
    )
    return pl.pallas_call(
        _attn_kernel,
        grid_spec=grid_spec,
        out_shape=jax.ShapeDtypeStruct((B * S, ATTN_Q_W), BF16),
        compiler_params=pltpu.CompilerParams(
            dimension_semantics=("arbitrary", "arbitrary"), vmem_limit_bytes=VMEM_LIMIT_BYTES),
        name="attn",
    )(sink, P, P, P, P, P, P, P, bias, bias, bias, qg, kg)


def _log_sigmoid(x):
    return jnp.minimum(x, 0.0) - jnp.log1p(jnp.exp(-jnp.abs(x)))


def _mlstm_kernel(q_ref, k_ref, vt_ref, o_ref, gt_ref, cwq_ref, cwk_ref, cbq_ref, cbk_ref, og_ref,
                  ym_ref,
                  pad_ref, qs_ref, ks_ref, v1t_ref, ls_ref, cs_ref, ct_ref, m_ref, hs_ref):
    L = MLSTM_L
    S = q_ref.shape[0]
    nc = S // L
    hh = pl.program_id(1)
    H = N_HEADS_MLSTM

    halo = SUBLANES
    zpad = jnp.zeros((halo, LANES), F32)
    pad_ref[0:halo, :] = zpad
    pad_ref[halo + S:2 * halo + S, :] = zpad

    def conv_silu(src_ref, w_ref, b_ref, dst_ref, scale):
        pad_ref[halo:halo + S, :] = src_ref[...].astype(F32)
        for c in range(nc):
            acc = jnp.broadcast_to(b_ref[...], (L, LANES))
            for j in range(CONV_WIDTH):
                off = c * L + halo - CONV_WIDTH // 2 + j
                acc = acc + pad_ref[off:off + L, :] * w_ref[j:j + 1, :]
            y = acc * jax.nn.sigmoid(acc)
            dst_ref[c * L:(c + 1) * L, :] = (y * scale).astype(BF16)

    conv_silu(q_ref, cwq_ref, cbq_ref, qs_ref, 1.0)
    conv_silu(k_ref, cwk_ref, cbk_ref, ks_ref, HEAD_DIM_MLSTM ** -0.5)

    ones_blk = jnp.ones((L, LANES), BF16)
    for c in range(nc):
        v1t_ref[c, 0:L, :] = vt_ref[c]
        v1t_ref[c, L:2 * L, :] = ones_blk

    g_all = gt_ref[...].reshape(nc * MLSTM_GATE_W, LANES)
    ls = _log_sigmoid(g_all)
    lane = lax.broadcasted_iota(I32, (1, LANES), 1)
    pre = ls
    suf = ls
    d = 1
    while d < LANES:
        pre = pre + jnp.where(lane >= d, pltpu.roll(pre, d, axis=1), 0.0)
        suf = suf + jnp.where(lane < LANES - d, pltpu.roll(suf, LANES - d, axis=1), 0.0)
        d *= 2
    ls_ref[...] = ls.reshape(nc, MLSTM_GATE_W, LANES)
    cs_ref[0] = pre.reshape(nc, MLSTM_GATE_W, LANES)
    cs_ref[1] = suf.reshape(nc, MLSTM_GATE_W, LANES)

    ct_ref[...] = jnp.zeros(ct_ref.shape, F32)
    m_ref[...] = jnp.zeros(m_ref.shape, F32)

    r_i = lax.broadcasted_iota(I32, (L, L), 0)
    c_i = lax.broadcasted_iota(I32, (L, L), 1)
    masks = (c_i <= r_i, c_i >= r_i)

    def chunk_step(dr, c, accumulate):
        mask = masks[dr]
        row_i = dr * 2 * H + hh
        row_f = dr * 2 * H + H + hh
        ig_row = gt_ref[c, pl.ds(row_i, 1), :]
        lf_row = ls_ref[c, pl.ds(row_f, 1), :]
        a_row = ig_row - cs_ref[dr, c, pl.ds(row_f, 1), :]
        am = jnp.where(mask, a_row, -jnp.inf)
        cmax = jnp.max(am, axis=1, keepdims=True)
        bcol = jnp.sum(jnp.where(mask, lf_row, 0.0), axis=1, keepdims=True)
        m_row = m_ref[dr, 0:1, :]
        mm = jnp.maximum(cmax, m_row)
        w = jnp.exp(am - mm)
        inter = jnp.exp(m_row - mm)
        clamp = jnp.exp(-(bcol + mm))
        row0 = pl.multiple_of(c * L, L)
        q_c = qs_ref[pl.ds(row0, L), :]
        k_c = ks_ref[pl.ds(row0, L), :]
        sw = (_dot_nt(q_c, k_c) * w).astype(BF16)
        qi = (q_c.astype(F32) * inter).astype(BF16)
        v1t = v1t_ref[c]
        ct = ct_ref[dr]
        lhs = jnp.concatenate([sw, qi], axis=1)
        rhs_t = jnp.concatenate([v1t, ct.astype(BF16)], axis=1)
        nd = _dot_nt(lhs, rhs_t)
        hval = nd[:, 0:L] / jnp.maximum(jnp.abs(nd[:, L:2 * L]), clamp)
        if accumulate:
            hs_ref[pl.ds(row0, L), :] = hs_ref[pl.ds(row0, L), :] + hval
        else:
            hs_ref[pl.ds(row0, L), :] = hval
        m_last = jnp.maximum(jnp.max(a_row, axis=1, keepdims=True), m_row)
        decay = jnp.exp(m_row - m_last)
        ws_row = jnp.exp(a_row - m_last)
        ut = _dot((v1t.astype(F32) * ws_row).astype(BF16), k_c)
        ct_ref[dr] = decay * ct + ut
        m_new = jnp.sum(lf_row, axis=1, keepdims=True) + m_last
        m_ref[dr] = jnp.broadcast_to(m_new, (SUBLANES, LANES))

    def first_half(i, carry):
        chunk_step(0, i, False)
        chunk_step(1, nc - 1 - i, False)
        return carry

    def second_half(i, carry):
        chunk_step(0, i, True)
        chunk_step(1, nc - 1 - i, True)
        return carry

    lax.fori_loop(0, nc // 2, first_half, 0, unroll=True)
    lax.fori_loop(nc // 2, nc, second_half, 0, unroll=True)

    for c in range(nc):
        hsum = hs_ref[c * L:(c + 1) * L, :]
        y = hsum * lax.rsqrt(jnp.mean(hsum * hsum, axis=-1, keepdims=True) + RMS_EPS) * og_ref[0]
        og = jax.nn.sigmoid(o_ref[c * L:(c + 1) * L, :].astype(F32))
        ym_ref[c * L:(c + 1) * L, :] = (y * og).astype(BF16)


def _mlstm(QKO, VT, GT, conv_w, conv_b, out_gain, B, S):
    H = N_HEADS_MLSTM
    L = MLSTM_L
    nc = S // L
    return pl.pallas_call(
        _mlstm_kernel,
        grid=(B, H),
        in_specs=[
            pl.BlockSpec((None, S, LANES), lambda b, h: (h, b, 0)),
            pl.BlockSpec((None, S, LANES), lambda b, h: (H + h, b, 0)),
            pl.BlockSpec((nc, LANES, LANES), lambda b, h: (b, h, 0)),
            pl.BlockSpec((None, S, LANES), lambda b, h: (2 * H + h, b, 0)),
            pl.BlockSpec((nc, MLSTM_GATE_W, LANES), lambda b, h: (b, 0, 0)),
            pl.BlockSpec((CONV_WIDTH, LANES), lambda b, h: (0, h)),
            pl.BlockSpec((CONV_WIDTH, LANES), lambda b, h: (0, H + h)),
            pl.BlockSpec((1, LANES), lambda b, h: (0, h)),
            pl.BlockSpec((1, LANES), lambda b, h: (0, H + h)),
            pl.BlockSpec((1, 1, LANES), lambda b, h: (h, 0, 0)),
        ],
        out_specs=pl.BlockSpec((S, LANES), lambda b, h: (b, h)),
        out_shape=jax.ShapeDtypeStruct((B * S, MLSTM_W), BF16),
        scratch_shapes=[
            pltpu.VMEM((S + 2 * SUBLANES, LANES), F32),
            pltpu.VMEM((S, LANES), BF16),
            pltpu.VMEM((S, LANES), BF16),
            pltpu.VMEM((nc, 2 * L, LANES), BF16),
            pltpu.VMEM((nc, MLSTM_GATE_W, LANES), F32),
            pltpu.VMEM((2, nc, MLSTM_GATE_W, LANES), F32),
            pltpu.VMEM((2, 2 * L, LANES), F32),
            pltpu.VMEM((2, SUBLANES, LANES), F32),
            pltpu.VMEM((S, LANES), F32),
        ],
        compiler_params=pltpu.CompilerParams(
            dimension_semantics=("arbitrary", "arbitrary"), vmem_limit_bytes=VMEM_LIMIT_BYTES),
        name="mlstm",
    )(QKO, QKO, VT, QKO, GT, conv_w, conv_w, conv_b, conv_b, out_gain[:, None, :])


def _merge_kernel(ya_ref, ym_ref, mga_ref, mgm_ref, mb_ref, x_ref, wa_ref, wm_ref, wo_ref, g2_ref,
                  wr_ref, br_ref, tri_ref, x1_ref, h2_ref, eid_ref, wt_ref, cnt_ref, run_ref):
    tm = x_ref.shape[0]

    @pl.when(pl.program_id(0) == 0)
    def _():
        run_ref[...] = jnp.zeros(run_ref.shape, F32)
    a = _dot(ya_ref[...], wa_ref[...])
    mm = _dot(ym_ref[...], wm_ref[...])
    ga = jax.nn.sigmoid(mga_ref[...].astype(F32) + mb_ref[:, 0:D_MODEL])
    gm = jax.nn.sigmoid(mgm_ref[...].astype(F32) + mb_ref[:, D_MODEL:2 * D_MODEL])
    u = (ga * a + gm * mm).astype(BF16)
    x1 = x_ref[...] + _dot(u, wo_ref[...])
    x1_ref[...] = x1
    h2 = x1 * lax.rsqrt(jnp.mean(x1 * x1, axis=-1, keepdims=True) + RMS_EPS) * g2_ref[...]
    for j in range(D_MODEL // LANES):
        h2_ref[pl.ds(j, tm, stride=ROW_TILE), :] = h2[:, j * LANES:(j + 1) * LANES]

    lg = _dot_nt(wr_ref[...], h2.astype(BF16)) + br_ref[...]
    row = lax.broadcasted_iota(I32, (SUBLANES, tm), 0)
    gl = jnp.where(row < N_GROUPS, lg[0:SUBLANES], -jnp.inf)
    gmax = jnp.max(gl, axis=0, keepdims=True)
    g_idx = jnp.min(jnp.where(gl == gmax, row, SUBLANES), axis=0, keepdims=True)
    p_group = 1.0 / jnp.sum(jnp.exp(gl - gmax), axis=0, keepdims=True)
    e_in = lg[SUBLANES * N_GROUPS:SUBLANES * (N_GROUPS + 1)]
    for g in range(N_GROUPS - 2, -1, -1):
        e_in = jnp.where(g_idx == g, lg[SUBLANES * (g + 1):SUBLANES * (g + 2)], e_in)
    v1 = jnp.max(e_in, axis=0, keepdims=True)
    i1 = jnp.min(jnp.where(e_in == v1, row, SUBLANES), axis=0, keepdims=True)
    e2 = jnp.where(row == i1, -jnp.inf, e_in)
    v2 = jnp.max(e2, axis=0, keepdims=True)
    i2 = jnp.min(jnp.where(e2 == v2, row, SUBLANES), axis=0, keepdims=True)
    t = jnp.exp(v2 - v1)
    w1 = p_group / (1.0 + t)
    w2 = p_group * t / (1.0 + t)
    id1 = g_idx * EXPERTS_PER_GROUP + i1
    id2 = g_idx * EXPERTS_PER_GROUP + i2
    wt_ref[...] = jnp.where(row == 0, w1, jnp.where(row == 1, w2, 0.0))

    erow = lax.broadcasted_iota(I32, (N_EXPERTS, tm), 0)
    oh1 = (erow == id1).astype(F32)
    oh2 = (erow == id2).astype(F32)
    c1 = _dot(oh1.astype(BF16), tri_ref[...])
    c2 = _dot(oh2.astype(BF16), tri_ref[...])
    tot1 = jnp.sum(oh1, axis=1, keepdims=True)
    tot2 = jnp.sum(oh2, axis=1, keepdims=True)
    base = run_ref[:, 0:1]
    r1 = jnp.sum(oh1 * (base + c1), axis=0, keepdims=True)
    r2 = jnp.sum(oh2 * (base + tot1 + c2), axis=0, keepdims=True)
    run_ref[...] = run_ref[...] + (tot1 + tot2)
    cnt_ref[...] = run_ref[...]
    eid_ref[...] = jnp.where(row == 0, id1, jnp.where(row == 1, id2, jnp.where(
        row == 2, r1.astype(I32), jnp.where(row == 3, r2.astype(I32), 0))))


def _merge(ya, ym, P, merge_b, x2, w_a, w_m, w_o, g2, w_r, b_r):
    T = x2.shape[0]
    tm = TOK_TILE
    const = lambda i: (0, 0)
    ti = jnp.arange(tm, dtype=I32)
    tri = (ti[:, None] < ti[None, :]).astype(BF16)
    return pl.pallas_call(
        _merge_kernel,
        grid=(T // tm,),
        in_specs=[
            pl.BlockSpec((tm, ATTN_Q_W), lambda i: (i, 0)),
            pl.BlockSpec((tm, MLSTM_W), lambda i: (i, 0)),
            pl.BlockSpec((tm, D_MODEL), lambda i: (i, 0)),
            pl.BlockSpec((tm, D_MODEL), lambda i: (i, 1)),
            pl.BlockSpec((1, 2 * D_MODEL), const),
            pl.BlockSpec((tm, D_MODEL), lambda i: (i, 0)),
            pl.BlockSpec((ATTN_Q_W, D_MODEL), const),
            pl.BlockSpec((MLSTM_W, D_MODEL), const),
            pl.BlockSpec((D_MODEL, D_MODEL), const),
            pl.BlockSpec((1, D_MODEL), const),
            pl.BlockSpec((LANES, D_MODEL), const),
            pl.BlockSpec((LANES, 1), const),
            pl.BlockSpec((tm, tm), const),
        ],
        out_specs=[
            pl.BlockSpec((tm, D_MODEL), lambda i: (i, 0)),
            pl.BlockSpec((tm * ROW_TILE, LANES), lambda i: (i, 0)),
            pl.BlockSpec((SUBLANES, tm), lambda i: (0, i)),
            pl.BlockSpec((SUBLANES, tm), lambda i: (0, i)),
            pl.BlockSpec((N_EXPERTS, LANES), const),
        ],
        out_shape=[
            jax.ShapeDtypeStruct((T, D_MODEL), F32),
            jax.ShapeDtypeStruct((T * ROW_TILE, LANES), F32),
            jax.ShapeDtypeStruct((SUBLANES, T), I32),
            jax.ShapeDtypeStruct((SUBLANES, T), F32),
            jax.ShapeDtypeStruct((N_EXPERTS, LANES), F32),
        ],
        scratch_shapes=[pltpu.VMEM((N_EXPERTS, LANES), F32)],
        compiler_params=pltpu.CompilerParams(
            dimension_semantics=("arbitrary",), vmem_limit_bytes=VMEM_LIMIT_BYTES),
        name="merge",
    )(ya, ym, P, P, merge_b, x2, w_a, w_m, w_o, g2, w_r, b_r, tri)


def _inv_kernel(dest_ref, pad_hbm, inv_hbm, inv_smem, sem, *, chunk):
    c = pl.program_id(0)

    @pl.when(c == 0)
    def _():
        cp = pltpu.make_async_copy(pad_hbm, inv_smem, sem)
        cp.start()
        cp.wait()

    def body(i, carry):
        inv_smem[dest_ref[0, 0, i]] = c * chunk + i
        return carry
    lax.fori_loop(0, chunk, body, 0, unroll=64)

    @pl.when(c == pl.num_programs(0) - 1)
    def _():
        cp = pltpu.make_async_copy(inv_smem, inv_hbm, sem)
        cp.start()
        cp.wait()


def _inverse_map(dest_flat, n_rows, pad_base):
    A = dest_flat.shape[0]
    chunk = min(A, 4096)
    nch = A // chunk
    pad_ids = pad_base + (jnp.arange(n_rows, dtype=I32) & (2 * MOE_BLK - 1))
    return pl.pallas_call(
        functools.partial(_inv_kernel, chunk=chunk),
        grid=(nch,),
        in_specs=[
            pl.BlockSpec((1, 1, chunk), lambda c: (c, 0, 0), memory_space=pltpu.SMEM),
            pl.BlockSpec(memory_space=pl.ANY),
        ],
        out_specs=pl.BlockSpec(memory_space=pl.ANY),
        scratch_shapes=[pltpu.SMEM((n_rows,), I32), pltpu.SemaphoreType.DMA],
        out_shape=jax.ShapeDtypeStruct((n_rows,), I32),
        compiler_params=pltpu.CompilerParams(dimension_semantics=("arbitrary",)),
        name="inverse_map",
    )(dest_flat.reshape(nch, 1, chunk), pad_ids)


def _moe_kernel(blk_e_ref, nused_ref,
                invp_ref, invc_ref, invn_ref, h2_hbm, wg_ref, wu_ref, wd_ref,
                y_hbm,
                xg_ref, ys_ref, xs_ref, wgb_ref, wub_ref, wdb_ref, gsem, ssem, *, n_tok):
    blk = MOE_BLK
    b = pl.program_id(0)
    nused = nused_ref[0]
    slot = lax.rem(b, 2)
    oslot = 1 - slot
    tok_mask = n_tok - 1
    n_lane_tiles = D_MODEL // LANES

    def gather_copy(r, a, slot_):
        return pltpu.make_async_copy(
            h2_hbm.at[pl.ds((a & tok_mask) * ROW_TILE, ROW_TILE), :],
            xg_ref.at[slot_, pl.ds(r * ROW_TILE, ROW_TILE), :],
            gsem.at[slot_])

    def scatter_copy(r, a, slot_):
        return pltpu.make_async_copy(
            ys_ref.at[slot_, pl.ds(r * ROW_TILE, ROW_TILE), :],
            y_hbm.at[pl.ds(a * ROW_TILE, ROW_TILE), :],
            ssem.at[slot_])

    def wait_rows(copy_fn, slot_):
        for _ in range(blk):
            copy_fn(0, 0, slot_).wait()

    @pl.when(b < nused)
    def _():
        @pl.when(b == 0)
        def _():
            ys_ref[...] = jnp.zeros(ys_ref.shape, F32)

            def prologue(r, carry):
                scatter_copy(r, TOP_K * n_tok + r, 0).start()
                gather_copy(r, invc_ref[0, 0, r], 0).start()
                return carry
            lax.fori_loop(0, blk, prologue, 0)

        changed = (b == 0) | (blk_e_ref[b] != blk_e_ref[jnp.maximum(b - 1, 0)])

        @pl.when(changed)
        def _():
            wgb_ref[...] = wg_ref[0].astype(BF16)
            wub_ref[...] = wu_ref[0].astype(BF16)
            wdb_ref[...] = wd_ref[0].astype(BF16)

        wait_rows(gather_copy, slot)
        per = blk // n_lane_tiles
        for j in range(n_lane_tiles):
            xs_ref[:, j * LANES:(j + 1) * LANES] = (
                xg_ref[slot, pl.ds(j, blk, stride=ROW_TILE), :].astype(BF16))
            for r in range(j * per, (j + 1) * per):
                gather_copy(r, invn_ref[0, 0, r], oslot).start(priority=r % 2)
        x = xs_ref[...]
        nw = 256
        acts = []
        r0 = 0
        n_up = D_EXPERT // nw
        n_dn = D_MODEL // nw
        per_up = blk // (2 * n_up)
        per_dn = (blk - n_up * per_up) // n_dn
        for n in range(n_up):
            g = _dot(x, wgb_ref[:, n * nw:(n + 1) * nw])
            u = _dot(x, wub_ref[:, n * nw:(n + 1) * nw])
            acts.append((g * jax.nn.sigmoid(g) * u).astype(BF16))
            for r in range(r0, r0 + per_up):
                scatter_copy(r, invp_ref[0, 0, r], oslot).start(priority=r % 2)
            r0 += per_up
        act = jnp.concatenate(acts, axis=1)
        wait_rows(scatter_copy, slot)
        for n in range(n_dn):
            y = _dot(act, wdb_ref[:, n * nw:(n + 1) * nw])
            for jj in range(nw // LANES):
                j = n * (nw // LANES) + jj
                ys_ref[slot, pl.ds(j, blk, stride=ROW_TILE), :] = y[:, jj * LANES:(jj + 1) * LANES]
            for r in range(r0, r0 + per_dn):
                scatter_copy(r, invp_ref[0, 0, r], oslot).start(priority=r % 2)
            r0 += per_dn
        assert r0 == blk

        @pl.when(b == nused - 1)
        def _():
            def epilogue(r, carry):
                scatter_copy(r, invc_ref[0, 0, r], slot).start()
                return carry
            lax.fori_loop(0, blk, epilogue, 0)
            wait_rows(scatter_copy, oslot)
            wait_rows(scatter_copy, slot)
            wait_rows(gather_copy, oslot)


def _moe(blk_e, nused, inv, h2g, w_gate, w_up, w_down, T):
    blk = MOE_BLK
    assert T & (T - 1) == 0, "token count must be a power of two for the id -> token mask"
    nblk = inv.shape[0] // blk
    inv3 = inv.reshape(nblk, 1, blk)
    virt = (TOP_K * T + blk + jnp.arange(blk, dtype=I32)).reshape(1, 1, blk)
    invp3 = jnp.concatenate([virt, inv3[:-1]], axis=0)
    grid_spec = pltpu.PrefetchScalarGridSpec(
        num_scalar_prefetch=2,
        grid=(nblk,),
        in_specs=[
            pl.BlockSpec((1, 1, blk), lambda b, e, n: (b, 0, 0), memory_space=pltpu.SMEM),
            pl.BlockSpec((1, 1, blk), lambda b, e, n: (b, 0, 0), memory_space=pltpu.SMEM),
            pl.BlockSpec((1, 1, blk), lambda b, e, n: (jnp.minimum(b + 1, nblk - 1), 0, 0),
                         memory_space=pltpu.SMEM),
            pl.BlockSpec(memory_space=pl.ANY),
            pl.BlockSpec((1, D_MODEL, D_EXPERT), lambda b, e, n: (e[b], 0, 0)),
            pl.BlockSpec((1, D_MODEL, D_EXPERT), lambda b, e, n: (e[b], 0, 0)),
            pl.BlockSpec((1, D_EXPERT, D_MODEL), lambda b, e, n: (e[b], 0, 0)),
        ],
        out_specs=pl.BlockSpec(memory_space=pl.ANY),
        scratch_shapes=[
            pltpu.VMEM((2, blk * ROW_TILE, LANES), F32),
            pltpu.VMEM((2, blk * ROW_TILE, LANES), F32),
            pltpu.VMEM((blk, D_MODEL), BF16),
            pltpu.VMEM((D_MODEL, D_EXPERT), BF16),
            pltpu.VMEM((D_MODEL, D_EXPERT), BF16),
            pltpu.VMEM((D_EXPERT, D_MODEL), BF16),
            pltpu.SemaphoreType.DMA((2,)),
            pltpu.SemaphoreType.DMA((2,)),
        ],
    )
    return pl.pallas_call(
        functools.partial(_moe_kernel, n_tok=T),
        grid_spec=grid_spec,
        out_shape=jax.ShapeDtypeStruct(((TOP_K * T + 2 * blk) * ROW_TILE, LANES), F32),
        compiler_params=pltpu.CompilerParams(
            dimension_semantics=("arbitrary",), vmem_limit_bytes=VMEM_LIMIT_BYTES),
        name="moe",
    )(blk_e, nused, invp3, inv3, inv3, h2g, w_gate, w_up, w_down)


def _combine_kernel(x1_ref, y0_ref, y1_ref, wt_ref, o_ref):
    tm = x1_ref.shape[0]
    r_i = lax.broadcasted_iota(I32, (LANES, LANES), 0)
    c_i = lax.broadcasted_iota(I32, (LANES, LANES), 1)
    eye = r_i == c_i
    for c in range(tm // LANES):
        rows = pl.ds(c * LANES, LANES)
        w0 = jnp.sum(jnp.where(eye, wt_ref[0:1, c * LANES:(c + 1) * LANES], 0.0), axis=1, keepdims=True)
        w1 = jnp.sum(jnp.where(eye, wt_ref[1:2, c * LANES:(c + 1) * LANES], 0.0), axis=1, keepdims=True)
        for j in range(D_MODEL // LANES):
            sl = slice(j * LANES, (j + 1) * LANES)
            tile = pl.ds(c * LANES * ROW_TILE + j, LANES, stride=ROW_TILE)
            o_ref[rows, sl] = x1_ref[rows, sl] + w0 * y0_ref[tile, :] + w1 * y1_ref[tile, :]


def _combine(x1, Y, wts):
    T = x1.shape[0]
    tm = TOK_TILE
    nt = T // tm
    return pl.pallas_call(
        _combine_kernel,
        grid=(nt,),
        in_specs=[
            pl.BlockSpec((tm, D_MODEL), lambda i: (i, 0)),
            pl.BlockSpec((tm * ROW_TILE, LANES), lambda i: (i, 0)),
            pl.BlockSpec((tm * ROW_TILE, LANES), lambda i: (i + nt, 0)),
            pl.BlockSpec((SUBLANES, tm), lambda i: (0, i)),
        ],
        out_specs=pl.BlockSpec((tm, D_MODEL), lambda i: (i, 0)),
        out_shape=jax.ShapeDtypeStruct((T, D_MODEL), F32),
        compiler_params=pltpu.CompilerParams(
            dimension_semantics=("arbitrary",), vmem_limit_bytes=VMEM_LIMIT_BYTES),
        name="combine",
    )(x1, Y, Y, wts)


def _t5_bucket(rel):
    half = N_REL_BUCKETS // 2
    max_exact = half // 2
    sign = jnp.where(rel > 0, half, 0)
    n = jnp.abs(rel)
    nf = jnp.maximum(n, 1).astype(jnp.float32)
    large = max_exact + (jnp.log(nf / max_exact) / math.log(REL_MAX_DISTANCE / max_exact)
                         * (half - max_exact)).astype(jnp.int32)
    large = jnp.minimum(large, half - 1)
    return sign + jnp.where(n < max_exact, n, large)


def _attn_bias(rel_table):
    qi = jnp.arange(ATTN_BLOCK)
    ks = jnp.arange(3 * ATTN_BLOCK) - ATTN_BLOCK
    rel = ks[None, :] - qi[:, None]
    onehot = jax.nn.one_hot(_t5_bucket(rel), N_REL_BUCKETS, dtype=F32)
    bias = jnp.einsum('qsb,bh->hqs', onehot, rel_table.astype(F32),
                      precision=lax.Precision.HIGHEST)
    bias = jnp.where((jnp.abs(rel) <= WINDOW)[None], bias, NEG_INF)
    not_prev = (ks >= 0)[None, None, :]
    not_next = (ks < ATTN_BLOCK)[None, None, :]
    return jnp.stack([bias,
                      jnp.where(not_prev, bias, NEG_INF),
                      jnp.where(not_next, bias, NEG_INF),
                      jnp.where(not_prev & not_next, bias, NEG_INF)], axis=0)


def _block_table(counts, T):
    blk = MOE_BLK
    nblk = (TOP_K * T) // blk + N_EXPERTS
    pcounts = ((counts + blk - 1) // blk) * blk
    pends = jnp.cumsum(pcounts)
    pstarts = (pends - pcounts).astype(I32)
    nused = (pends[-1] // blk).astype(I32)
    first_row = jnp.arange(nblk, dtype=I32) * blk
    blk_e = jnp.sum((first_row[:, None] >= pends[None, :]).astype(I32), axis=1)
    blk_e = jnp.minimum(blk_e, N_EXPERTS - 1)
    last_e = jnp.sum((((nused - 1) * blk) >= pends).astype(I32))
    blk_e = jnp.where(jnp.arange(nblk) < nused, blk_e, jnp.minimum(last_e, N_EXPERTS - 1)).astype(I32)
    return pstarts, blk_e, nused.reshape(1), nblk * blk


def kernel(x, norm1_g, w_in, attn_q_gain, attn_k_gain, attn_sink, rel_bias_table, mlstm_conv_w,
           mlstm_conv_b, mlstm_gate_b, mlstm_out_gain, w_branch_attn, w_branch_mlstm, merge_b, w_out,
           norm2_g, w_router_group, b_router_group, w_router_expert, b_router_expert,
           w_expert_gate, w_expert_up, w_expert_down):
    B, S, D = x.shape
    T = B * S
    depth = norm1_g.shape[0]
    x2 = x.reshape(T, D)
    bias = _attn_bias(rel_bias_table)
    for l in range(depth):
        w = w_in[l]
        w_main = jnp.concatenate([w[:, _OFF_MG:_W_IN], w[:, _OFF_QA:_OFF_VM], w[:, _OFF_OM:_OFF_GM]],
                                 axis=1).astype(BF16)
        w_vt = w[:, _OFF_VM:_OFF_OM].T.astype(BF16)
        w_gt = w[:, _OFF_GM:_OFF_MG].T.astype(BF16)
        gate_b = mlstm_gate_b[l].reshape(MLSTM_GATE_W, 1).astype(F32)
        P, QKO, VT, GT = _in_proj(x2, norm1_g[l][None, :], w_main, w_vt, w_gt, gate_b)

        qg = jnp.tile(attn_q_gain[l], LANES // HEAD_DIM_ATTN)[None, :]
        kg = jnp.tile(attn_k_gain[l], LANES // HEAD_DIM_ATTN)[None, :]
        ya = _attention(P, bias, attn_sink[l].astype(F32), qg, kg, B, S)

        ym = _mlstm(QKO, VT, GT, mlstm_conv_w[l], mlstm_conv_b[l][None, :], mlstm_out_gain[l], B, S)

        wr = jnp.zeros((LANES, D), F32)
        wr = wr.at[0:N_GROUPS].set(w_router_group[l].T)
        wr = wr.at[SUBLANES:SUBLANES + N_EXPERTS].set(w_router_expert[l].T)
        b_r = jnp.zeros((LANES,), F32)
        b_r = b_r.at[0:N_GROUPS].set(b_router_group[l])
        b_r = b_r.at[SUBLANES:SUBLANES + N_EXPERTS].set(b_router_expert[l])
        x1, h2g, route, wts, cnt = _merge(
            ya, ym, P, merge_b[l][None, :], x2, w_branch_attn[l].astype(BF16),
            w_branch_mlstm[l].astype(BF16), w_out[l].astype(BF16), norm2_g[l][None, :],
            wr.astype(BF16), b_r[:, None])

        pstarts, blk_e, nused, n_rows = _block_table(cnt[:, 0].astype(I32), T)
        eid_flat = route[0:TOP_K].reshape(TOP_K * T)
        seg_start = jnp.sum(jnp.where(eid_flat[None, :] == jnp.arange(N_EXPERTS, dtype=I32)[:, None],
                                      pstarts[:, None], 0), axis=0)
        dest = seg_start + route[TOP_K:2 * TOP_K].reshape(TOP_K * T)
        inv = _inverse_map(dest, n_rows, TOP_K * T)
        Y = _moe(blk_e, nused, inv, h2g, w_expert_gate[l], w_expert_up[l], w_expert_down[l], T)
        x2 = _combine(x1, Y, wts)
    return x2.reshape(B, S, D)
```

```python
import functools
import math

import jax
import jax.numpy as jnp
from jax import lax
from jax.experimental import pallas as pl
from jax.experimental.pallas import tpu as pltpu

F32 = jnp.float32
BF16 = jnp.bfloat16
I32 = jnp.int32

LANES = 128
SUBLANES = 8
VMEM_LIMIT_BYTES = 56 * 1024 * 1024

D_MODEL = 1024
N_HEADS_ATTN = 8
N_KV_HEADS = 2
HEAD_DIM_ATTN = 64
WINDOW = 128
ATTN_BLOCK = WINDOW
N_REL_BUCKETS = 32
REL_MAX_DISTANCE = 128
N_HEADS_MLSTM = 4
HEAD_DIM_MLSTM = 128
CONV_WIDTH = 5
ATTN_Q_W = N_HEADS_ATTN * HEAD_DIM_ATTN
ATTN_KV_W = N_KV_HEADS * HEAD_DIM_ATTN
MLSTM_W = N_HEADS_MLSTM * HEAD_DIM_MLSTM
MLSTM_GATE_W = 2 * 2 * N_HEADS_MLSTM
N_GROUPS = 4
EXPERTS_PER_GROUP = 8
N_EXPERTS = N_GROUPS * EXPERTS_PER_GROUP
TOP_K = 2
D_EXPERT = 512
RMS_EPS = 1e-6
NEG_INF = -1e30

_OFF_QA = 0
_OFF_VM = ATTN_Q_W + 2 * ATTN_KV_W + 2 * MLSTM_W
_OFF_OM = _OFF_VM + MLSTM_W
_OFF_GM = _OFF_OM + MLSTM_W
_OFF_MG = _OFF_GM + MLSTM_GATE_W
_W_IN = _OFF_MG + 2 * D_MODEL

P_MG = 0
P_QA = 2 * D_MODEL
P_KA = P_QA + ATTN_Q_W
P_VA = P_KA + ATTN_KV_W
P_QM = P_VA + ATTN_KV_W
P_KM = P_QM + MLSTM_W
P_OM = P_KM + MLSTM_W
P_W = P_OM + MLSTM_W

ATTN_ROWS = 32
MLSTM_L = 128
TOK_TILE = 512
MOE_BLK = 256
ROW_TILE = SUBLANES

_NT = (((1,), (1,)), ((), ()))


def _dot(a, b):
    return jnp.dot(a, b, preferred_element_type=F32)


def _dot_nt(a, b):
    return lax.dot_general(a, b, _NT, preferred_element_type=F32)


def _inproj_kernel(x_ref, g1_ref, wm_ref, wvt_ref, wgt_ref, gb_ref, p_ref, qko_ref, vt_ref, gt_ref):
    x = x_ref[...]
    h = x * lax.rsqrt(jnp.mean(x * x, axis=-1, keepdims=True) + RMS_EPS) * g1_ref[...]
    hb = h.astype(BF16)
    step = 512
    for c0 in range(0, P_QM, step):
        w = min(step, P_QM - c0)
        p_ref[:, c0:c0 + w] = _dot(hb, wm_ref[:, c0:c0 + w]).astype(BF16)
    for c0 in range(P_QM, P_W, step):
        res = _dot(hb, wm_ref[:, c0:c0 + step]).astype(BF16)
        for jj in range(step // LANES):
            qko_ref[(c0 - P_QM) // LANES + jj] = res[:, jj * LANES:(jj + 1) * LANES]
    vt = _dot_nt(wvt_ref[...], hb)
    gt = _dot_nt(wgt_ref[...], hb) + gb_ref[...]
    for cc in range(x.shape[0] // LANES):
        vt_ref[cc] = vt[:, cc * LANES:(cc + 1) * LANES].astype(BF16)
        gt_ref[cc] = gt[:, cc * LANES:(cc + 1) * LANES]


def _in_proj(x2, g1, w_main, w_vt, w_gt, gate_b):
    T = x2.shape[0]
    tm = TOK_TILE
    nck = tm // LANES
    return pl.pallas_call(
        _inproj_kernel,
        grid=(T // tm,),
        in_specs=[
            pl.BlockSpec((tm, D_MODEL), lambda i: (i, 0)),
            pl.BlockSpec((1, D_MODEL), lambda i: (0, 0)),
            pl.BlockSpec((D_MODEL, P_W), lambda i: (0, 0)),
            pl.BlockSpec((MLSTM_W, D_MODEL), lambda i: (0, 0)),
            pl.BlockSpec((MLSTM_GATE_W, D_MODEL), lambda i: (0, 0)),
            pl.BlockSpec((MLSTM_GATE_W, 1), lambda i: (0, 0)),
        ],
        out_specs=[
            pl.BlockSpec((tm, P_QM), lambda i: (i, 0)),
            pl.BlockSpec((3 * N_HEADS_MLSTM, tm, LANES), lambda i: (0, i, 0)),
            pl.BlockSpec((nck, MLSTM_W, LANES), lambda i: (i, 0, 0)),
            pl.BlockSpec((nck, MLSTM_GATE_W, LANES), lambda i: (i, 0, 0)),
        ],
        out_shape=[
            jax.ShapeDtypeStruct((T, P_QM), BF16),
            jax.ShapeDtypeStruct((3 * N_HEADS_MLSTM, T, LANES), BF16),
            jax.ShapeDtypeStruct((T // LANES, MLSTM_W, LANES), BF16),
            jax.ShapeDtypeStruct((T // LANES, MLSTM_GATE_W, LANES), F32),
        ],
        compiler_params=pltpu.CompilerParams(
            dimension_semantics=("arbitrary",), vmem_limit_bytes=VMEM_LIMIT_BYTES),
        name="in_proj",
    )(x2, g1, w_main, w_vt, w_gt, gate_b)


ATTN_PAIR = 4


def _attn_kernel(sink_ref, q_ref, kp_ref, kc_ref, kn_ref, vp_ref, vc_ref, vn_ref,
                 bias_first_ref, bias_mid_ref, bias_last_ref, qg_ref, kg_ref, o_ref):
    half = HEAD_DIM_ATTN
    Q = ATTN_BLOCK
    G = ATTN_PAIR
    nk = (G + 2) * Q
    n_tiles = ATTN_Q_W // LANES
    group = N_HEADS_ATTN // N_KV_HEADS
    r_i = lax.broadcasted_iota(I32, (LANES, LANES), 0)
    c_i = lax.broadcasted_iota(I32, (LANES, LANES), 1)
    bd = ((r_i // half) == (c_i // half)).astype(BF16)
    lane = lax.broadcasted_iota(I32, (1, LANES), 1)
    lo_mask = lane < half

    parts = [kp_ref[...], kc_ref[...], kn_ref[...]] + [q_ref[:, j * LANES:(j + 1) * LANES] for j in range(n_tiles)]
    t = jnp.concatenate(parts, axis=0).astype(F32)
    tt = t * t
    hi = tt.astype(BF16)
    lo = (tt - hi.astype(F32)).astype(BF16)
    nrow = t.shape[0]
    ms2 = _dot(jnp.concatenate([hi, lo], axis=0), bd)
    ms = (ms2[0:nrow] + ms2[nrow:2 * nrow]) * (1.0 / half)
    tn = t * lax.rsqrt(ms + RMS_EPS)
    kn = tn[0:nk] * kg_ref[...]
    k_nat = kn.astype(BF16)
    k_rot = pltpu.roll(kn, half, axis=1).astype(BF16)
    qn = [tn[nk + j * G * Q:nk + (j + 1) * G * Q] * qg_ref[...] * (HEAD_DIM_ATTN ** -0.5)
          for j in range(n_tiles)]

    v = jnp.concatenate([vp_ref[...], vc_ref[...], vn_ref[...]], axis=0).astype(F32)
    v_rot = pltpu.roll(v, half, axis=1)
    zero = jnp.zeros_like(v)
    v_lo = [jnp.where(lo_mask, v, zero).astype(BF16), jnp.where(lo_mask, v_rot, zero).astype(BF16)]
    v_hi = [jnp.where(lo_mask, zero, v_rot).astype(BF16), jnp.where(lo_mask, zero, v).astype(BF16)]

    def q_masked(hd):
        keep = lo_mask if hd % 2 == 0 else ~lo_mask
        return jnp.where(keep, qn[hd // 2], 0.0).astype(BF16)

    heads_nat = [hd for hd in range(N_HEADS_ATTN) if (hd // group) == (hd % 2)]
    heads_rot = [hd for hd in range(N_HEADS_ATTN) if (hd // group) != (hd % 2)]
    PB = 2
    assert G % PB == 0
    qm = {hd: q_masked(hd) for hd in range(N_HEADS_ATTN)}
    scores = {}
    for kx, hds in ((k_nat, heads_nat), (k_rot, heads_rot)):
        for p in range(G // PB):
            prow = slice(p * PB * Q, (p + 1) * PB * Q)
            s_all = _dot_nt(jnp.concatenate([qm[hd][prow] for hd in hds], axis=0),
                            kx[p * PB * Q:(p * PB + PB + 2) * Q])
            for i, hd in enumerate(hds):
                scores[hd, p] = s_all[i * PB * Q:(i + 1) * PB * Q]

    bias_refs = (bias_first_ref,) + (bias_mid_ref,) * (G - 2) + (bias_last_ref,)
    for g in range(G):
        rows = slice(g * Q, (g + 1) * Q)
        win = slice(g * Q, (g + 3) * Q)
        p, gl = divmod(g, PB)
        for j in range(n_tiles):
            probs = []
            for hd in (2 * j, 2 * j + 1):
                s = scores[hd, p][gl * Q:(gl + 1) * Q, gl * Q:(gl + 3) * Q] + bias_refs[g][0, hd]
                sink = sink_ref[hd]
                m = jnp.maximum(jnp.max(s, axis=1, keepdims=True), sink)
                e = jnp.exp(s - m)
                den = jnp.sum(e, axis=1, keepdims=True) + jnp.exp(sink - m)
                probs.append((e * (1.0 / den)).astype(BF16))
            kv = (2 * j) // group
            v_pair = jnp.concatenate([v_lo[kv][win], v_hi[kv][win]], axis=0)
            o_ref[rows, j * LANES:(j + 1) * LANES] = _dot(jnp.concatenate(probs, axis=1), v_pair).astype(BF16)


def _attention(P, bias, sink, qg, kg, B, S):
    Q = ATTN_BLOCK
    G = ATTN_PAIR
    nb = S // Q
    assert G >= 2 and nb % G == 0
    npair = nb // G
    qcol = P_QA // ATTN_Q_W
    kcol = P_KA // ATTN_KV_W
    vcol = P_VA // ATTN_KV_W

    def edge(col, shift):
        def imap(b, n, sink_ref):
            nn = jnp.clip(n * G + shift, 0, nb - 1)
            return (b * nb + nn, col)
        return pl.BlockSpec((Q, ATTN_KV_W), imap)

    def own(col):
        return pl.BlockSpec((G * Q, ATTN_KV_W), lambda b, n, s: (b * npair + n, col))

    def bias_spec(imap):
        return pl.BlockSpec((1, N_HEADS_ATTN, Q, 3 * Q), imap)

    grid_spec = pltpu.PrefetchScalarGridSpec(
        num_scalar_prefetch=1,
        grid=(B, npair),
        in_specs=[
            pl.BlockSpec((G * Q, ATTN_Q_W), lambda b, n, s: (b * npair + n, qcol)),
            edge(kcol, -1), own(kcol), edge(kcol, G),
            edge(vcol, -1), own(vcol), edge(vcol, G),
            bias_spec(lambda b, n, s: ((n == 0).astype(I32), 0, 0, 0)),
            bias_spec(lambda b, n, s: (0, 0, 0, 0)),
            bias_spec(lambda b, n, s: (2 * (n == npair - 1).astype(I32), 0, 0, 0)),
            pl.BlockSpec((1, LANES), lambda b, n, s: (0, 0)),
            pl.BlockSpec((1, LANES), lambda b, n, s: (0, 0)),
        ],
        out_specs=pl.BlockSpec((G * Q, ATTN_Q_W), lambda b, n, s: (b * npair + n, 0)),
    )
    return pl.pallas_call(
        _attn_kernel,
        grid_spec=grid_spec,
        out_shape=jax.ShapeDtypeStruct((B * S, ATTN_Q_W), BF16),
        compiler_params=pltpu.CompilerParams(
            dimension_semantics=("arbitrary", "arbitrary"), vmem_limit_bytes=VMEM_LIMIT_BYTES),
        name="attn",
    )(sink, P, P, P, P, P, P, P, bias, bias, bias, qg, kg)


def _log_sigmoid(x):
    return jnp.minimum(x, 0.0) - jnp.log1p(jnp.exp(-jnp.abs(x)))


def _mlstm_kernel(q_ref, k_ref, vt_ref, o_ref, gt_ref, cwq_ref, cwk_ref, cbq_ref, cbk_ref, og_ref,
                  ym_ref,
                  pad_ref, qs_ref, ks_ref, v1t_ref, ls_ref, cs_ref, ct_ref, m_ref, hs_ref):
    L = MLSTM_L
    S = q_ref.shape[0]
    nc = S // L
    hh = pl.program_id(1)
    H = N_HEADS_MLSTM

    halo = SUBLANES
    zpad = jnp.zeros((halo, LANES), F32)
    pad_ref[0:halo, :] = zpad
    pad_ref[halo + S:2 * halo + S, :] = zpad

    def conv_silu(src_ref, w_ref, b_ref, dst_ref, scale):
        pad_ref[halo:halo + S, :] = src_ref[...].astype(F32)
        for c in range(nc):
            acc = jnp.broadcast_to(b_ref[...], (L, LANES))
            for j in range(CONV_WIDTH):
                off = c * L + halo - CONV_WIDTH // 2 + j
                acc = acc + pad_ref[off:off + L, :] * w_ref[j:j + 1, :]
            y = acc * jax.nn.sigmoid(acc)
            dst_ref[c * L:(c + 1) * L, :] = (y * scale).astype(BF16)

    conv_silu(q_ref, cwq_ref, cbq_ref, qs_ref, 1.0)
    conv_silu(k_ref, cwk_ref, cbk_ref, ks_ref, HEAD_DIM_MLSTM ** -0.5)

    ones_blk = jnp.ones((L, LANES), BF16)
    for c in range(nc):
        v1t_ref[c, 0:L, :] = vt_ref[c]
        v1t_ref[c, L:2 * L, :] = ones_blk

    g_all = gt_ref[...].reshape(nc * MLSTM_GATE_W, LANES)
    ls = _log_sigmoid(g_all)
    lane = lax.broadcasted_iota(I32, (1, LANES), 1)
    pre = ls
    suf = ls
    d = 1
    while d < LANES:
        pre = pre + jnp.where(lane >= d, pltpu.roll(pre, d, axis=1), 0.0)
        suf = suf + jnp.where(lane < LANES - d, pltpu.roll(suf, LANES - d, axis=1), 0.0)
        d *= 2
    ls_ref[...] = ls.reshape(nc, MLSTM_GATE_W, LANES)
    cs_ref[0] = pre.reshape(nc, MLSTM_GATE_W, LANES)
    cs_ref[1] = suf.reshape(nc, MLSTM_GATE_W, LANES)

    ct_ref[...] = jnp.zeros(ct_ref.shape, F32)
    m_ref[...] = jnp.zeros(m_ref.shape, F32)

    r_i = lax.broadcasted_iota(I32, (L, L), 0)
    c_i = lax.broadcasted_iota(I32, (L, L), 1)
    masks = (c_i <= r_i, c_i >= r_i)

    def chunk_step(dr, c, accumulate):
        mask = masks[dr]
        row_i = dr * 2 * H + hh
        row_f = dr * 2 * H + H + hh
        ig_row = gt_ref[c, pl.ds(row_i, 1), :]
        lf_row = ls_ref[c, pl.ds(row_f, 1), :]
        a_row = ig_row - cs_ref[dr, c, pl.ds(row_f, 1), :]
        am = jnp.where(mask, a_row, -jnp.inf)
        cmax = jnp.max(am, axis=1, keepdims=True)
        bcol = jnp.sum(jnp.where(mask, lf_row, 0.0), axis=1, keepdims=True)
        m_row = m_ref[dr, 0:1, :]
        mm = jnp.maximum(cmax, m_row)
        w = jnp.exp(am - mm)
        inter = jnp.exp(m_row - mm)
        clamp = jnp.exp(-(bcol + mm))
        row0 = pl.multiple_of(c * L, L)
        q_c = qs_ref[pl.ds(row0, L), :]
        k_c = ks_ref[pl.ds(row0, L), :]
        sw = (_dot_nt(q_c, k_c) * w).astype(BF16)
        qi = (q_c.astype(F32) * inter).astype(BF16)
        v1t = v1t_ref[c]
        ct = ct_ref[dr]
        lhs = jnp.concatenate([sw, qi], axis=1)
        rhs_t = jnp.concatenate([v1t, ct.astype(BF16)], axis=1)
        nd = _dot_nt(lhs, rhs_t)
        hval = nd[:, 0:L] / jnp.maximum(jnp.abs(nd[:, L:2 * L]), clamp)
        if accumulate:
            hs_ref[pl.ds(row0, L), :] = hs_ref[pl.ds(row0, L), :] + hval
        else:
            hs_ref[pl.ds(row0, L), :] = hval
        m_last = jnp.maximum(jnp.max(a_row, axis=1, keepdims=True), m_row)
        decay = jnp.exp(m_row - m_last)
        ws_row = jnp.exp(a_row - m_last)
        ut = _dot((v1t.astype(F32) * ws_row).astype(BF16), k_c)
        ct_ref[dr] = decay * ct + ut
        m_new = jnp.sum(lf_row, axis=1, keepdims=True) + m_last
        m_ref[dr] = jnp.broadcast_to(m_new, (SUBLANES, LANES))

    def first_half(i, carry):
        chunk_step(0, i, False)
        chunk_step(1, nc - 1 - i, False)
        return carry

    def second_half(i, carry):
        chunk_step(0, i, True)
        chunk_step(1, nc - 1 - i, True)
        return carry

    lax.fori_loop(0, nc // 2, first_half, 0, unroll=True)
    lax.fori_loop(nc // 2, nc, second_half, 0, unroll=True)

    for c in range(nc):
        hsum = hs_ref[c * L:(c + 1) * L, :]
        y = hsum * lax.rsqrt(jnp.mean(hsum * hsum, axis=-1, keepdims=True) + RMS_EPS) * og_ref[0]
        og = jax.nn.sigmoid(o_ref[c * L:(c + 1) * L, :].astype(F32))
        ym_ref[c * L:(c + 1) * L, :] = (y * og).astype(BF16)


def _mlstm(QKO, VT, GT, conv_w, conv_b, out_gain, B, S):
    H = N_HEADS_MLSTM
    L = MLSTM_L
    nc = S // L
    return pl.pallas_call(
        _mlstm_kernel,
        grid=(B, H),
        in_specs=[
            pl.BlockSpec((None, S, LANES), lambda b, h: (h, b, 0)),
            pl.BlockSpec((None, S, LANES), lambda b, h: (H + h, b, 0)),
            pl.BlockSpec((nc, LANES, LANES), lambda b, h: (b, h, 0)),
            pl.BlockSpec((None, S, LANES), lambda b, h: (2 * H + h, b, 0)),
            pl.BlockSpec((nc, MLSTM_GATE_W, LANES), lambda b, h: (b, 0, 0)),
            pl.BlockSpec((CONV_WIDTH, LANES), lambda b, h: (0, h)),
            pl.BlockSpec((CONV_WIDTH, LANES), lambda b, h: (0, H + h)),
            pl.BlockSpec((1, LANES), lambda b, h: (0, h)),
            pl.BlockSpec((1, LANES), lambda b, h: (0, H + h)),
            pl.BlockSpec((1, 1, LANES), lambda b, h: (h, 0, 0)),
        ],
        out_specs=pl.BlockSpec((S, LANES), lambda b, h: (b, h)),
        out_shape=jax.ShapeDtypeStruct((B * S, MLSTM_W), BF16),
        scratch_shapes=[
            pltpu.VMEM((S + 2 * SUBLANES, LANES), F32),
            pltpu.VMEM((S, LANES), BF16),
            pltpu.VMEM((S, LANES), BF16),
            pltpu.VMEM((nc, 2 * L, LANES), BF16),
            pltpu.VMEM((nc, MLSTM_GATE_W, LANES), F32),
            pltpu.VMEM((2, nc, MLSTM_GATE_W, LANES), F32),
            pltpu.VMEM((2, 2 * L, LANES), F32),
            pltpu.VMEM((2, SUBLANES, LANES), F32),
            pltpu.VMEM((S, LANES), F32),
        ],
        compiler_params=pltpu.CompilerParams(
            dimension_semantics=("arbitrary", "arbitrary"), vmem_limit_bytes=VMEM_LIMIT_BYTES),
        name="mlstm",
    )(QKO, QKO, VT, QKO, GT, conv_w, conv_w, conv_b, conv_b, out_gain[:, None, :])


def _merge_kernel(ya_ref, ym_ref, mga_ref, mgm_ref, mb_ref, x_ref, wa_ref, wm_ref, wo_ref, g2_ref,
                  wr_ref, br_ref, tri_ref, x1_ref, h2_ref, eid_ref, wt_ref, cnt_ref, run_ref):
    tm = x_ref.shape[0]

    @pl.when(pl.program_id(0) == 0)
    def _():
        run_ref[...] = jnp.zeros(run_ref.shape, F32)
    a = _dot(ya_ref[...], wa_ref[...])
    mm = _dot(ym_ref[...], wm_ref[...])
    ga = jax.nn.sigmoid(mga_ref[...].astype(F32) + mb_ref[:, 0:D_MODEL])
    gm = jax.nn.sigmoid(mgm_ref[...].astype(F32) + mb_ref[:, D_MODEL:2 * D_MODEL])
    u = (ga * a + gm * mm).astype(BF16)
    x1 = x_ref[...] + _dot(u, wo_ref[...])
    x1_ref[...] = x1
    h2 = x1 * lax.rsqrt(jnp.mean(x1 * x1, axis=-1, keepdims=True) + RMS_EPS) * g2_ref[...]
    for j in range(D_MODEL // LANES):
        h2_ref[pl.ds(j, tm, stride=ROW_TILE), :] = h2[:, j * LANES:(j + 1) * LANES]

    lg = _dot_nt(wr_ref[...], h2.astype(BF16)) + br_ref[...]
    row = lax.broadcasted_iota(I32, (SUBLANES, tm), 0)
    gl = jnp.where(row < N_GROUPS, lg[0:SUBLANES], -jnp.inf)
    gmax = jnp.max(gl, axis=0, keepdims=True)
    g_idx = jnp.min(jnp.where(gl == gmax, row, SUBLANES), axis=0, keepdims=True)
    p_group = 1.0 / jnp.sum(jnp.exp(gl - gmax), axis=0, keepdims=True)
    e_in = lg[SUBLANES * N_GROUPS:SUBLANES * (N_GROUPS + 1)]
    for g in range(N_GROUPS - 2, -1, -1):
        e_in = jnp.where(g_idx == g, lg[SUBLANES * (g + 1):SUBLANES * (g + 2)], e_in)
    v1 = jnp.max(e_in, axis=0, keepdims=True)
    i1 = jnp.min(jnp.where(e_in == v1, row, SUBLANES), axis=0, keepdims=True)
    e2 = jnp.where(row == i1, -jnp.inf, e_in)
    v2 = jnp.max(e2, axis=0, keepdims=True)
    i2 = jnp.min(jnp.where(e2 == v2, row, SUBLANES), axis=0, keepdims=True)
    t = jnp.exp(v2 - v1)
    w1 = p_group / (1.0 + t)
    w2 = p_group * t / (1.0 + t)
    id1 = g_idx * EXPERTS_PER_GROUP + i1
    id2 = g_idx * EXPERTS_PER_GROUP + i2
    wt_ref[...] = jnp.where(row == 0, w1, jnp.where(row == 1, w2, 0.0))

    erow = lax.broadcasted_iota(I32, (N_EXPERTS, tm), 0)
    oh1 = (erow == id1).astype(F32)
    oh2 = (erow == id2).astype(F32)
    c1 = _dot(oh1.astype(BF16), tri_ref[...])
    c2 = _dot(oh2.astype(BF16), tri_ref[...])
    tot1 = jnp.sum(oh1, axis=1, keepdims=True)
    tot2 = jnp.sum(oh2, axis=1, keepdims=True)
    base = run_ref[:, 0:1]
    r1 = jnp.sum(oh1 * (base + c1), axis=0, keepdims=True)
    r2 = jnp.sum(oh2 * (base + tot1 + c2), axis=0, keepdims=True)
    run_ref[...] = run_ref[...] + (tot1 + tot2)
    cnt_ref[...] = run_ref[...]
    eid_ref[...] = jnp.where(row == 0, id1, jnp.where(row == 1, id2, jnp.where(
        row == 2, r1.astype(I32), jnp.where(row == 3, r2.astype(I32), 0))))


def _merge(ya, ym, P, merge_b, x2, w_a, w_m, w_o, g2, w_r, b_r):
    T = x2.shape[0]
    tm = TOK_TILE
    const = lambda i: (0, 0)
    ti = jnp.arange(tm, dtype=I32)
    tri = (ti[:, None] < ti[None, :]).astype(BF16)
    return pl.pallas_call(
        _merge_kernel,
        grid=(T // tm,),
        in_specs=[
            pl.BlockSpec((tm, ATTN_Q_W), lambda i: (i, 0)),
            pl.BlockSpec((tm, MLSTM_W), lambda i: (i, 0)),
            pl.BlockSpec((tm, D_MODEL), lambda i: (i, 0)),
            pl.BlockSpec((tm, D_MODEL), lambda i: (i, 1)),
            pl.BlockSpec((1, 2 * D_MODEL), const),
            pl.BlockSpec((tm, D_MODEL), lambda i: (i, 0)),
            pl.BlockSpec((ATTN_Q_W, D_MODEL), const),
            pl.BlockSpec((MLSTM_W, D_MODEL), const),
            pl.BlockSpec((D_MODEL, D_MODEL), const),
            pl.BlockSpec((1, D_MODEL), const),
            pl.BlockSpec((LANES, D_MODEL), const),
            pl.BlockSpec((LANES, 1), const),
            pl.BlockSpec((tm, tm), const),
        ],
        out_specs=[
            pl.BlockSpec((tm, D_MODEL), lambda i: (i, 0)),
            pl.BlockSpec((tm * ROW_TILE, LANES), lambda i: (i, 0)),
            pl.BlockSpec((SUBLANES, tm), lambda i: (0, i)),
            pl.BlockSpec((SUBLANES, tm), lambda i: (0, i)),
            pl.BlockSpec((N_EXPERTS, LANES), const),
        ],
        out_shape=[
            jax.ShapeDtypeStruct((T, D_MODEL), F32),
            jax.ShapeDtypeStruct((T * ROW_TILE, LANES), F32),
            jax.ShapeDtypeStruct((SUBLANES, T), I32),
            jax.ShapeDtypeStruct((SUBLANES, T), F32),
            jax.ShapeDtypeStruct((N_EXPERTS, LANES), F32),
        ],
        scratch_shapes=[pltpu.VMEM((N_EXPERTS, LANES), F32)],
        compiler_params=pltpu.CompilerParams(
            dimension_semantics=("arbitrary",), vmem_limit_bytes=VMEM_LIMIT_BYTES),
        name="merge",
    )(ya, ym, P, P, merge_b, x2, w_a, w_m, w_o, g2, w_r, b_r, tri)


def _inv_kernel(dest_ref, pad_hbm, inv_hbm, inv_smem, sem, *, chunk):
    c = pl.program_id(0)

    @pl.when(c == 0)
    def _():
        cp = pltpu.make_async_copy(pad_hbm, inv_smem, sem)
        cp.start()
        cp.wait()

    def body(i, carry):
        inv_smem[dest_ref[0, 0, i]] = c * chunk + i
        return carry
    lax.fori_loop(0, chunk, body, 0, unroll=64)

    @pl.when(c == pl.num_programs(0) - 1)
    def _():
        cp = pltpu.make_async_copy(inv_smem, inv_hbm, sem)
        cp.start()
        cp.wait()


def _inverse_map(dest_flat, n_rows, pad_base):
    A = dest_flat.shape[0]
    chunk = min(A, 4096)
    nch = A // chunk
    pad_ids = pad_base + (jnp.arange(n_rows, dtype=I32) & (2 * MOE_BLK - 1))
    return pl.pallas_call(
        functools.partial(_inv_kernel, chunk=chunk),
        grid=(nch,),
        in_specs=[
            pl.BlockSpec((1, 1, chunk), lambda c: (c, 0, 0), memory_space=pltpu.SMEM),
            pl.BlockSpec(memory_space=pl.ANY),
        ],
        out_specs=pl.BlockSpec(memory_space=pl.ANY),
        scratch_shapes=[pltpu.SMEM((n_rows,), I32), pltpu.SemaphoreType.DMA],
        out_shape=jax.ShapeDtypeStruct((n_rows,), I32),
        compiler_params=pltpu.CompilerParams(dimension_semantics=("arbitrary",)),
        name="inverse_map",
    )(dest_flat.reshape(nch, 1, chunk), pad_ids)


def _moe_kernel(blk_e_ref, nused_ref,
                invp_ref, invc_ref, invn_ref, h2_hbm, wg_ref, wu_ref, wd_ref,
                y_hbm,
                xg_ref, ys_ref, xs_ref, wgb_ref, wub_ref, wdb_ref, gsem, ssem, *, n_tok):
    blk = MOE_BLK
    b = pl.program_id(0)
    nused = nused_ref[0]
    slot = lax.rem(b, 2)
    oslot = 1 - slot
    tok_mask = n_tok - 1
    n_lane_tiles = D_MODEL // LANES

    def gather_copy(r, a, slot_):
        return pltpu.make_async_copy(
            h2_hbm.at[pl.ds((a & tok_mask) * ROW_TILE, ROW_TILE), :],
            xg_ref.at[slot_, pl.ds(r * ROW_TILE, ROW_TILE), :],
            gsem.at[slot_])

    def scatter_copy(r, a, slot_):
        return pltpu.make_async_copy(
            ys_ref.at[slot_, pl.ds(r * ROW_TILE, ROW_TILE), :],
            y_hbm.at[pl.ds(a * ROW_TILE, ROW_TILE), :],
            ssem.at[slot_])

    def wait_rows(copy_fn, slot_):
        for _ in range(blk):
            copy_fn(0, 0, slot_).wait()

    @pl.when(b < nused)
    def _():
        @pl.when(b == 0)
        def _():
            ys_ref[...] = jnp.zeros(ys_ref.shape, F32)

            def prologue(r, carry):
                scatter_copy(r, TOP_K * n_tok + r, 0).start()
                gather_copy(r, invc_ref[0, 0, r], 0).start()
                return carry
            lax.fori_loop(0, blk, prologue, 0)

        changed = (b == 0) | (blk_e_ref[b] != blk_e_ref[jnp.maximum(b - 1, 0)])

        @pl.when(changed)
        def _():
            wgb_ref[...] = wg_ref[0].astype(BF16)
            wub_ref[...] = wu_ref[0].astype(BF16)
            wdb_ref[...] = wd_ref[0].astype(BF16)

        wait_rows(gather_copy, slot)
        per = blk // n_lane_tiles
        for j in range(n_lane_tiles):
            xs_ref[:, j * LANES:(j + 1) * LANES] = (
                xg_ref[slot, pl.ds(j, blk, stride=ROW_TILE), :].astype(BF16))
            for r in range(j * per, (j + 1) * per):
                gather_copy(r, invn_ref[0, 0, r], oslot).start(priority=r % 2)
        x = xs_ref[...]
        nw = 256
        acts = []
        r0 = 0
        n_up = D_EXPERT // nw
        n_dn = D_MODEL // nw
        per_up = blk // (2 * n_up)
        per_dn = (blk - n_up * per_up) // n_dn
        for n in range(n_up):
            g = _dot(x, wgb_ref[:, n * nw:(n + 1) * nw])
            u = _dot(x, wub_ref[:, n * nw:(n + 1) * nw])
            acts.append((g * jax.nn.sigmoid(g) * u).astype(BF16))
            for r in range(r0, r0 + per_up):
                scatter_copy(r, invp_ref[0, 0, r], oslot).start(priority=r % 2)
            r0 += per_up
        act = jnp.concatenate(acts, axis=1)
        wait_rows(scatter_copy, slot)
        for n in range(n_dn):
            y = _dot(act, wdb_ref[:, n * nw:(n + 1) * nw])
            for jj in range(nw // LANES):
                j = n * (nw // LANES) + jj
                ys_ref[slot, pl.ds(j, blk, stride=ROW_TILE), :] = y[:, jj * LANES:(jj + 1) * LANES]
            for r in range(r0, r0 + per_dn):
                scatter_copy(r, invp_ref[0, 0, r], oslot).start(priority=r % 2)
            r0 += per_dn
        assert r0 == blk

        @pl.when(b == nused - 1)
        def _():
            def epilogue(r, carry):
                scatter_copy(r, invc_ref[0, 0, r], slot).start()
                return carry
            lax.fori_loop(0, blk, epilogue, 0)
            wait_rows(scatter_copy, oslot)
            wait_rows(scatter_copy, slot)
            wait_rows(gather_copy, oslot)


def _moe(blk_e, nused, inv, h2g, w_gate, w_up, w_down, T):
    blk = MOE_BLK
    assert T & (T - 1) == 0, "token count must be a power of two for the id -> token mask"
    nblk = inv.shape[0] // blk
    inv3 = inv.reshape(nblk, 1, blk)
    virt = (TOP_K * T + blk + jnp.arange(blk, dtype=I32)).reshape(1, 1, blk)
    invp3 = jnp.concatenate([virt, inv3[:-1]], axis=0)
    grid_spec = pltpu.PrefetchScalarGridSpec(
        num_scalar_prefetch=2,
        grid=(nblk,),
        in_specs=[
            pl.BlockSpec((1, 1, blk), lambda b, e, n: (b, 0, 0), memory_space=pltpu.SMEM),
            pl.BlockSpec((1, 1, blk), lambda b, e, n: (b, 0, 0), memory_space=pltpu.SMEM),
            pl.BlockSpec((1, 1, blk), lambda b, e, n: (jnp.minimum(b + 1, nblk - 1), 0, 0),
                         memory_space=pltpu.SMEM),
            pl.BlockSpec(memory_space=pl.ANY),
            pl.BlockSpec((1, D_MODEL, D_EXPERT), lambda b, e, n: (e[b], 0, 0)),
            pl.BlockSpec((1, D_MODEL, D_EXPERT), lambda b, e, n: (e[b], 0, 0)),
            pl.BlockSpec((1, D_EXPERT, D_MODEL), lambda b, e, n: (e[b], 0, 0)),
        ],
        out_specs=pl.BlockSpec(memory_space=pl.ANY),
        scratch_shapes=[
            pltpu.VMEM((2, blk * ROW_TILE, LANES), F32),
            pltpu.VMEM((2, blk * ROW_TILE, LANES), F32),
            pltpu.VMEM((blk, D_MODEL), BF16),
            pltpu.VMEM((D_MODEL, D_EXPERT), BF16),
            pltpu.VMEM((D_MODEL, D_EXPERT), BF16),
            pltpu.VMEM((D_EXPERT, D_MODEL), BF16),
            pltpu.SemaphoreType.DMA((2,)),
            pltpu.SemaphoreType.DMA((2,)),
        ],
    )
    return pl.pallas_call(
        functools.partial(_moe_kernel, n_tok=T),
        grid_spec=grid_spec,
        out_shape=jax.ShapeDtypeStruct(((TOP_K * T + 2 * blk) * ROW_TILE, LANES), F32),
        compiler_params=pltpu.CompilerParams(
            dimension_semantics=("arbitrary",), vmem_limit_bytes=VMEM_LIMIT_BYTES),
        name="moe",
    )(blk_e, nused, invp3, inv3, inv3, h2g, w_gate, w_up, w_down)


def _combine_kernel(x1_ref, y0_ref, y1_ref, wt_ref, o_ref):
    tm = x1_ref.shape[0]
    r_i = lax.broadcasted_iota(I32, (LANES, LANES), 0)
    c_i = lax.broadcasted_iota(I32, (LANES, LANES), 1)
    eye = r_i == c_i
    for c in range(tm // LANES):
        rows = pl.ds(c * LANES, LANES)
        w0 = jnp.sum(jnp.where(eye, wt_ref[0:1, c * LANES:(c + 1) * LANES], 0.0), axis=1, keepdims=True)
        w1 = jnp.sum(jnp.where(eye, wt_ref[1:2, c * LANES:(c + 1) * LANES], 0.0), axis=1, keepdims=True)
        for j in range(D_MODEL // LANES):
            sl = slice(j * LANES, (j + 1) * LANES)
            tile = pl.ds(c * LANES * ROW_TILE + j, LANES, stride=ROW_TILE)
            o_ref[rows, sl] = x1_ref[rows, sl] + w0 * y0_ref[tile, :] + w1 * y1_ref[tile, :]


def _combine(x1, Y, wts):
    T = x1.shape[0]
    tm = TOK_TILE
    nt = T // tm
    return pl.pallas_call(
        _combine_kernel,
        grid=(nt,),
        in_specs=[
            pl.BlockSpec((tm, D_MODEL), lambda i: (i, 0)),
            pl.BlockSpec((tm * ROW_TILE, LANES), lambda i: (i, 0)),
            pl.BlockSpec((tm * ROW_TILE, LANES), lambda i: (i + nt, 0)),
            pl.BlockSpec((SUBLANES, tm), lambda i: (0, i)),
        ],
        out_specs=pl.BlockSpec((tm, D_MODEL), lambda i: (i, 0)),
        out_shape=jax.ShapeDtypeStruct((T, D_MODEL), F32),
        compiler_params=pltpu.CompilerParams(
            dimension_semantics=("arbitrary",), vmem_limit_bytes=VMEM_LIMIT_BYTES),
        name="combine",
    )(x1, Y, Y, wts)


def _t5_bucket(rel):
    half = N_REL_BUCKETS // 2
    max_exact = half // 2
    sign = jnp.where(rel > 0, half, 0)
    n = jnp.abs(rel)
    nf = jnp.maximum(n, 1).astype(jnp.float32)
    large = max_exact + (jnp.log(nf / max_exact) / math.log(REL_MAX_DISTANCE / max_exact)
                         * (half - max_exact)).astype(jnp.int32)
    large = jnp.minimum(large, half - 1)
    return sign + jnp.where(n < max_exact, n, large)


def _attn_bias(rel_table):
    qi = jnp.arange(ATTN_BLOCK)
    ks = jnp.arange(3 * ATTN_BLOCK) - ATTN_BLOCK
    rel = ks[None, :] - qi[:, None]
    onehot = jax.nn.one_hot(_t5_bucket(rel), N_REL_BUCKETS, dtype=F32)
    bias = jnp.einsum('qsb,bh->hqs', onehot, rel_table.astype(F32),
                      precision=lax.Precision.HIGHEST)
    bias = jnp.where((jnp.abs(rel) <= WINDOW)[None], bias, NEG_INF)
    not_prev = (ks >= 0)[None, None, :]
    not_next = (ks < ATTN_BLOCK)[None, None, :]
    return jnp.stack([bias,
                      jnp.where(not_prev, bias, NEG_INF),
                      jnp.where(not_next, bias, NEG_INF),
                      jnp.where(not_prev & not_next, bias, NEG_INF)], axis=0)


def _block_table(counts, T):
    blk = MOE_BLK
    nblk = (TOP_K * T) // blk + N_EXPERTS
    pcounts = ((counts + blk - 1) // blk) * blk
    pends = jnp.cumsum(pcounts)
    pstarts = (pends - pcounts).astype(I32)
    nused = (pends[-1] // blk).astype(I32)
    first_row = jnp.arange(nblk, dtype=I32) * blk
    blk_e = jnp.sum((first_row[:, None] >= pends[None, :]).astype(I32), axis=1)
    blk_e = jnp.minimum(blk_e, N_EXPERTS - 1)
    last_e = jnp.sum((((nused - 1) * blk) >= pends).astype(I32))
    blk_e = jnp.where(jnp.arange(nblk) < nused, blk_e, jnp.minimum(last_e, N_EXPERTS - 1)).astype(I32)
    return pstarts, blk_e, nused.reshape(1), nblk * blk


def kernel(x, norm1_g, w_in, attn_q_gain, attn_k_gain, attn_sink, rel_bias_table, mlstm_conv_w,
           mlstm_conv_b, mlstm_gate_b, mlstm_out_gain, w_branch_attn, w_branch_mlstm, merge_b, w_out,
           norm2_g, w_router_group, b_router_group, w_router_expert, b_router_expert,
           w_expert_gate, w_expert_up, w_expert_down):
    B, S, D = x.shape
    T = B * S
    depth = norm1_g.shape[0]
    x2 = x.reshape(T, D)
    bias = _attn_bias(rel_bias_table)
    for l in range(depth):
        w = w_in[l]
        w_main = jnp.concatenate([w[:, _OFF_MG:_W_IN], w[:, _OFF_QA:_OFF_VM], w[:, _OFF_OM:_OFF_GM]],
                                 axis=1).astype(BF16)
        w_vt = w[:, _OFF_VM:_OFF_OM].T.astype(BF16)
        w_gt = w[:, _OFF_GM:_OFF_MG].T.astype(BF16)
        gate_b = mlstm_gate_b[l].reshape(MLSTM_GATE_W, 1).astype(F32)
        P, QKO, VT, GT = _in_proj(x2, norm1_g[l][None, :], w_main, w_vt, w_gt, gate_b)

        qg = jnp.tile(attn_q_gain[l], LANES // HEAD_DIM_ATTN)[None, :]
        kg = jnp.tile(attn_k_gain[l], LANES // HEAD_DIM_ATTN)[None, :]
        ya = _attention(P, bias, attn_sink[l].astype(F32), qg, kg, B, S)

        ym = _mlstm(QKO, VT, GT, mlstm_conv_w[l], mlstm_conv_b[l][None, :], mlstm_out_gain[l], B, S)

        wr = jnp.zeros((LANES, D), F32)
        wr = wr.at[0:N_GROUPS].set(w_router_group[l].T)
        wr = wr.at[SUBLANES:SUBLANES + N_EXPERTS].set(w_router_expert[l].T)
        b_r = jnp.zeros((LANES,), F32)
        b_r = b_r.at[0:N_GROUPS].set(b_router_group[l])
        b_r = b_r.at[SUBLANES:SUBLANES + N_EXPERTS].set(b_router_expert[l])
        x1, h2g, route, wts, cnt = _merge(
            ya, ym, P, merge_b[l][None, :], x2, w_branch_attn[l].astype(BF16),
            w_branch_mlstm[l].astype(BF16), w_out[l].astype(BF16), norm2_g[l][None, :],
            wr.astype(BF16), b_r[:, None])

        pstarts, blk_e, nused, n_rows = _block_table(cnt[:, 0].astype(I32), T)
        eid_flat = route[0:TOP_K].reshape(TOP_K * T)
        seg_start = jnp.sum(jnp.where(eid_flat[None, :] == jnp.arange(N_EXPERTS, dtype=I32)[:, None],
                                      pstarts[:, None], 0), axis=0)
        dest = seg_start + route[TOP_K:2 * TOP_K].reshape(TOP_K * T)
        inv = _inverse_map(dest, n_rows, TOP_K * T)
        Y = _moe(blk_e, nused, inv, h2g, w_expert_gate[l], w_expert_up[l], w_expert_down[l], T)
        x2 = _combine(x1, Y, wts)
    return x2.reshape(B, S, D)
```
